```python
import jax, jax.numpy as jnp
from jax import lax
import numpy as np

D_MODEL = 2048
BATCH = 4
SEQ = 2048
DEPTH = 2
DEC_BATCH = 128
DEC_SEQ = 4
PAST_LEN = 2048
PAGE_SIZE = 128

HEAD_DIM = 128
H_MOBA = 6
H_SB = 6
H_MEM = 4
N_HEADS = H_MOBA + H_SB + H_MEM
W_MOBA = H_MOBA * HEAD_DIM
W_SB = H_SB * HEAD_DIM
W_MEM = H_MEM * HEAD_DIM
IN_WIDTH = 3 * W_MOBA + 3 * W_SB + W_MEM
MOBA_BLOCK = 256
MOBA_TOPK = 3
Q_BLOCK = 128
N_MEM = 256
D_FF = 5504
DEEPNORM_ALPHA = (2.0 * DEPTH) ** 0.25
DEEPNORM_BETA = (8.0 * DEPTH) ** -0.25
LN_EPS = 1e-5
RMS_EPS = 1e-6
NEG_INF = -1e30

kernel_name = 'moba_stickbreaking_memory_hymba_step'


def alibi_slopes(n):
    return jnp.exp2(-8.0 * jnp.arange(1, n + 1, dtype=jnp.float32) / n)


def layer_norm(x, g, b):
    xf = x.astype(jnp.float32)
    mu = jnp.mean(xf, axis=-1, keepdims=True)
    var = jnp.mean(jnp.square(xf - mu), axis=-1, keepdims=True)
    y = (xf - mu) * lax.rsqrt(var + LN_EPS)
    return (y * g.astype(jnp.float32) + b.astype(jnp.float32)).astype(x.dtype)


def swiglu(x, w_in, w_down):
    gate, up = jnp.split(x @ w_in, 2, axis=-1)
    return (jax.nn.silu(gate) * up) @ w_down


def macaron_half(x, w_in, w_down, g, b):
    return layer_norm(DEEPNORM_ALPHA * x + 0.5 * swiglu(x, w_in, w_down), g, b)


def project_in(h, w_in):
    bsz, t = h.shape[:2]
    cuts = [W_MOBA, 2 * W_MOBA, 3 * W_MOBA, 3 * W_MOBA + W_SB, 3 * W_MOBA + 2 * W_SB, 3 * W_MOBA + 3 * W_SB]
    parts = jnp.split(h @ w_in, cuts, axis=-1)
    return [p.reshape(bsz, t, -1, HEAD_DIM) for p in parts]


def mem_kv(mem, w_mem_kv):
    bsz, m = mem.shape[:2]
    mk, mv = jnp.split(mem @ w_mem_kv, 2, axis=-1)
    return mk.reshape(bsz, m, H_MEM, HEAD_DIM), mv.reshape(bsz, m, H_MEM, HEAD_DIM)


def pad_to_blocks(t):
    pad = (-t.shape[1]) % MOBA_BLOCK
    return jnp.pad(t, ((0, 0), (0, pad), (0, 0), (0, 0)))


def moba_attend(q, q_pos, k, v, slopes):
    tq = q.shape[0]
    n_blocks = k.shape[0] // MOBA_BLOCK
    kb = k.reshape(n_blocks, MOBA_BLOCK, H_MOBA, HEAD_DIM).transpose(2, 0, 1, 3)
    vb = v.reshape(n_blocks, MOBA_BLOCK, H_MOBA, HEAD_DIM).transpose(2, 0, 1, 3)
    k_mean = jnp.mean(kb.astype(jnp.float32), axis=2)
    qh = q.transpose(1, 0, 2)
    q_blk = q_pos // MOBA_BLOCK
    gate = jnp.einsum('htd,hnd->htn', qh.astype(jnp.float32), k_mean)
    fully_past = jnp.arange(n_blocks)[None, :] < q_blk[:, None]
    gate = jnp.where(fully_past[None], gate, NEG_INF)
    n_sel = min(MOBA_TOPK, n_blocks)
    _, sel = lax.top_k(gate, n_sel)
    sel_ok = sel < q_blk[None, :, None]
    h_idx = jnp.arange(H_MOBA)[:, None, None]
    k_sel = kb[h_idx, sel]
    v_sel = vb[h_idx, sel]
    k_own = kb[:, q_blk]
    v_own = vb[:, q_blk]
    offs = jnp.arange(MOBA_BLOCK, dtype=jnp.int32)
    pos_sel = sel[..., None] * MOBA_BLOCK + offs
    pos_own = q_blk[:, None] * MOBA_BLOCK + offs
    scale = HEAD_DIM ** -0.5
    s_sel = jnp.einsum('htd,htjsd->htjs', qh, k_sel).astype(jnp.float32) * scale
    s_sel = s_sel - slopes[:, None, None, None] * (q_pos[None, :, None, None] - pos_sel).astype(jnp.float32)
    s_sel = jnp.where(sel_ok[..., None], s_sel, NEG_INF)
    s_own = jnp.einsum('htd,htsd->hts', qh, k_own).astype(jnp.float32) * scale
    s_own = s_own - slopes[:, None, None] * (q_pos[:, None] - pos_own)[None].astype(jnp.float32)
    s_own = jnp.where((pos_own <= q_pos[:, None])[None], s_own, NEG_INF)
    scores = jnp.concatenate([s_sel.reshape(H_MOBA, tq, n_sel * MOBA_BLOCK), s_own], axis=-1)
    probs = jax.nn.softmax(scores, axis=-1).astype(v.dtype)
    v_cat = jnp.concatenate([v_sel.reshape(H_MOBA, tq, n_sel * MOBA_BLOCK, HEAD_DIM), v_own], axis=2)
    return jnp.einsum('hts,htsd->thd', probs, v_cat)


def moba_prompt(q, k, v, slopes):
    bsz, s = q.shape[:2]
    kp, vp = pad_to_blocks(k), pad_to_blocks(v)
    n_qb = s // Q_BLOCK
    qb = q.reshape(bsz, n_qb, Q_BLOCK, H_MOBA, HEAD_DIM)
    pos = jnp.arange(s, dtype=jnp.int32).reshape(n_qb, Q_BLOCK)

    def one_seq(args):
        q_s, k_s, v_s = args
        return lax.map(lambda a: moba_attend(a[0], a[1], k_s, v_s, slopes), (q_s, pos))

    out = lax.map(one_seq, (qb, kp, vp))
    return out.reshape(bsz, s, H_MOBA, HEAD_DIM)


def moba_sample(q, q_pos, k_full, v_full, slopes):
    kp, vp = pad_to_blocks(k_full), pad_to_blocks(v_full)
    return lax.map(lambda a: moba_attend(a[0], q_pos, a[1], a[2], slopes), (q, kp, vp))


def sb_attend(q, q_pos, k, v):
    z = jnp.einsum('bthd,bshd->bhts', q, k).astype(jnp.float32) * (HEAD_DIM ** -0.5)
    causal = (jnp.arange(k.shape[1], dtype=jnp.int32)[None, :] < q_pos[:, None])[None, None]
    log_beta = jax.nn.log_sigmoid(z)
    log_keep = jnp.where(causal, jax.nn.log_sigmoid(-z), 0.0)
    after = lax.cumsum(log_keep, axis=3, reverse=True) - log_keep
    weights = jnp.where(causal, jnp.exp(log_beta + after), 0.0)
    return jnp.einsum('bhts,bshd->bthd', weights.astype(v.dtype), v)


def sb_prompt(q, k, v):
    bsz, s = q.shape[:2]
    n_qb = s // Q_BLOCK
    qb = q.reshape(bsz, n_qb, Q_BLOCK, H_SB, HEAD_DIM).transpose(1, 0, 2, 3, 4)
    pos = jnp.arange(s, dtype=jnp.int32).reshape(n_qb, Q_BLOCK)
    out = lax.map(lambda a: sb_attend(a[0], a[1], k, v), (qb, pos))
    return out.transpose(1, 0, 2, 3, 4).reshape(bsz, s, H_SB, HEAD_DIM)


def mem_attend(q, mk, mv):
    scores = jnp.einsum('bthd,bmhd->bhtm', q, mk).astype(jnp.float32) * (HEAD_DIM ** -0.5)
    probs = jax.nn.softmax(scores, axis=-1).astype(mv.dtype)
    return jnp.einsum('bhtm,bmhd->bthd', probs, mv)


def mixer_out(o_moba, o_sb, o_mem, head_g, w_out):
    o = jnp.concatenate([o_moba, o_sb, o_mem], axis=2)
    of = o.astype(jnp.float32)
    of = of * lax.rsqrt(jnp.mean(jnp.square(of), axis=-1, keepdims=True) + RMS_EPS)
    bsz, t = o.shape[:2]
    o = of.reshape(bsz, t, N_HEADS * HEAD_DIM).astype(o.dtype) * head_g
    return o @ w_out


def gather_pages(pool, page_table):
    pages = pool[page_table]
    return pages.reshape(page_table.shape[0], page_table.shape[1] * pool.shape[1], pool.shape[2], pool.shape[3])


def setup_inputs(seed: int = 0) -> dict:
    key = jax.random.key(seed)
    ks = jax.random.split(key, 32)
    f32 = jnp.float32

    def nrm(k, shape, scale):
        return jax.random.normal(k, shape, f32) * scale

    n_pages = PAST_LEN // PAGE_SIZE
    n_used = DEC_BATCH * n_pages
    n_phys = (n_used * 5) // 4
    page_table = jax.random.permutation(ks[0], n_phys)[:n_used].reshape(DEC_BATCH, n_pages).astype(jnp.int32)
    d_in = D_MODEL ** -0.5
    return {
        'x_prompt': nrm(ks[1], (BATCH, SEQ, D_MODEL), 1.0),
        'x_sample': nrm(ks[2], (DEC_BATCH, DEC_SEQ, D_MODEL), 1.0),
        'cache_moba_k': nrm(ks[3], (DEPTH, n_phys, PAGE_SIZE, H_MOBA, HEAD_DIM), 1.0),
        'cache_moba_v': nrm(ks[4], (DEPTH, n_phys, PAGE_SIZE, H_MOBA, HEAD_DIM), 1.0),
        'cache_sb_k': nrm(ks[5], (DEPTH, n_phys, PAGE_SIZE, H_SB, HEAD_DIM), 1.0),
        'cache_sb_v': nrm(ks[6], (DEPTH, n_phys, PAGE_SIZE, H_SB, HEAD_DIM), 1.0),
        'cache_mem_k': nrm(ks[7], (DEPTH, DEC_BATCH, N_MEM, H_MEM, HEAD_DIM), 1.0),
        'cache_mem_v': nrm(ks[8], (DEPTH, DEC_BATCH, N_MEM, H_MEM, HEAD_DIM), 1.0),
        'page_table': page_table,
        'mem_prompt': nrm(ks[9], (BATCH, N_MEM, D_MODEL), 1.0),
        'ln1_g': 1.0 + nrm(ks[10], (DEPTH, D_MODEL), 0.02),
        'ln1_b': nrm(ks[11], (DEPTH, D_MODEL), 0.02),
        'ln2_g': 1.0 + nrm(ks[12], (DEPTH, D_MODEL), 0.02),
        'ln2_b': nrm(ks[13], (DEPTH, D_MODEL), 0.02),
        'ln3_g': 1.0 + nrm(ks[14], (DEPTH, D_MODEL), 0.02),
        'ln3_b': nrm(ks[15], (DEPTH, D_MODEL), 0.02),
        'ffn1_w_in': nrm(ks[16], (DEPTH, D_MODEL, 2 * D_FF), d_in),
        'ffn1_w_down': nrm(ks[17], (DEPTH, D_FF, D_MODEL), D_FF ** -0.5 * DEEPNORM_BETA),
        'w_in': nrm(ks[18], (DEPTH, D_MODEL, IN_WIDTH), d_in),
        'w_mem_kv': nrm(ks[19], (DEPTH, D_MODEL, 2 * W_MEM), d_in),
        'head_g': 1.0 + nrm(ks[20], (DEPTH, N_HEADS * HEAD_DIM), 0.02),
        'w_out': nrm(ks[21], (DEPTH, N_HEADS * HEAD_DIM, D_MODEL), (N_HEADS * HEAD_DIM) ** -0.5 * DEEPNORM_BETA),
        'ffn2_w_in': nrm(ks[22], (DEPTH, D_MODEL, 2 * D_FF), d_in),
        'ffn2_w_down': nrm(ks[23], (DEPTH, D_FF, D_MODEL), D_FF ** -0.5 * DEEPNORM_BETA),
    }


def reference(x_prompt, x_sample, cache_moba_k, cache_moba_v, cache_sb_k, cache_sb_v, cache_mem_k, cache_mem_v, page_table, mem_prompt, ln1_g, ln1_b, ln2_g, ln2_b, ln3_g, ln3_b, ffn1_w_in, ffn1_w_down, w_in, w_mem_kv, head_g, w_out, ffn2_w_in, ffn2_w_down):
    slopes = alibi_slopes(H_MOBA)
    xp, xs = x_prompt, x_sample
    pos_sample = PAST_LEN + jnp.arange(xs.shape[1], dtype=jnp.int32)
    mkp, mvp, skp, svp, memk, memv = [], [], [], [], [], []
    mks, mvs, sks, svs = [], [], [], []
    for l in range(DEPTH):
        xp = macaron_half(xp, ffn1_w_in[l], ffn1_w_down[l], ln1_g[l], ln1_b[l])
        xs = macaron_half(xs, ffn1_w_in[l], ffn1_w_down[l], ln1_g[l], ln1_b[l])

        qm, km, vm, qs, ks_, vs, qc = project_in(xp, w_in[l])
        mk, mv = mem_kv(mem_prompt, w_mem_kv[l])
        o = mixer_out(moba_prompt(qm, km, vm, slopes), sb_prompt(qs, ks_, vs), mem_attend(qc, mk, mv), head_g[l], w_out[l])
        xp = layer_norm(DEEPNORM_ALPHA * xp + o, ln2_g[l], ln2_b[l])
        mkp.append(km); mvp.append(vm); skp.append(ks_); svp.append(vs); memk.append(mk); memv.append(mv)

        qm2, km2, vm2, qs2, ks2, vs2, qc2 = project_in(xs, w_in[l])
        km_full = jnp.concatenate([gather_pages(cache_moba_k[l], page_table), km2], axis=1)
        vm_full = jnp.concatenate([gather_pages(cache_moba_v[l], page_table), vm2], axis=1)
        ks_full = jnp.concatenate([gather_pages(cache_sb_k[l], page_table), ks2], axis=1)
        vs_full = jnp.concatenate([gather_pages(cache_sb_v[l], page_table), vs2], axis=1)
        o2 = mixer_out(moba_sample(qm2, pos_sample, km_full, vm_full, slopes), sb_attend(qs2, pos_sample, ks_full, vs_full), mem_attend(qc2, cache_mem_k[l], cache_mem_v[l]), head_g[l], w_out[l])
        xs = layer_norm(DEEPNORM_ALPHA * xs + o2, ln2_g[l], ln2_b[l])
        mks.append(km2); mvs.append(vm2); sks.append(ks2); svs.append(vs2)

        xp = macaron_half(xp, ffn2_w_in[l], ffn2_w_down[l], ln3_g[l], ln3_b[l])
        xs = macaron_half(xs, ffn2_w_in[l], ffn2_w_down[l], ln3_g[l], ln3_b[l])

    return (xp, xs, jnp.stack(mkp), jnp.stack(mvp), jnp.stack(skp), jnp.stack(svp), jnp.stack(memk), jnp.stack(memv), jnp.stack(mks), jnp.stack(mvs), jnp.stack(sks), jnp.stack(svs))
```

```python
import functools

import jax
import jax.numpy as jnp
from jax import lax
from jax.experimental import pallas as pl
from jax.experimental.pallas import tpu as pltpu

F32 = jnp.float32
BF16 = jnp.bfloat16

D_MODEL = 2048
DEPTH = 2
HEAD_DIM = 128
H_MOBA = 6
H_SB = 6
H_MEM = 4
N_HEADS = H_MOBA + H_SB + H_MEM
W_MOBA = H_MOBA * HEAD_DIM
W_SB = H_SB * HEAD_DIM
W_MEM = H_MEM * HEAD_DIM
MOBA_BLOCK = 256
MOBA_TOPK = 3
PAGE_SIZE = 128
D_FF = 5504
DEEPNORM_ALPHA = (2.0 * DEPTH) ** 0.25
LN_EPS = 1e-5
RMS_EPS = 1e-6
NEG_INF = -1e30
SCALE = HEAD_DIM ** -0.5

LANES_V7X = 128
SUBLANES_V7X = 8
VMEM_BYTES_V7X = 64 * 1024 * 1024

FF_CHUNK = 512
D_FF_PAD = -(-D_FF // FF_CHUNK) * FF_CHUNK
ROW_TILE = 512
PROJ_GROUP = W_MOBA
SAMPLE_ROWS = SUBLANES_V7X
NEW_TILE = LANES_V7X


def _vmem_limit(nbytes):
    return int(min(nbytes + (8 << 20), VMEM_BYTES_V7X - (6 << 20)))


def _dot(a, b):
    return jnp.dot(a, b, preferred_element_type=F32)


def _dot_nt(a, b, precision=None):
    return lax.dot_general(a, b, (((1,), (1,)), ((), ())),
                           preferred_element_type=F32, precision=precision)


def _layer_norm(y, g, b):
    mu = jnp.mean(y, axis=-1, keepdims=True)
    d = y - mu
    var = jnp.mean(d * d, axis=-1, keepdims=True)
    return d * lax.rsqrt(var + LN_EPS) * g + b


def _rms_norm(o):
    return o * lax.rsqrt(jnp.mean(o * o, axis=-1, keepdims=True) + RMS_EPS)


def _log_sigmoid(z):
    return jnp.minimum(z, 0.0) - jnp.log1p(jnp.exp(-jnp.abs(z)))


def _suffix_matrix(n):
    j = lax.broadcasted_iota(jnp.int32, (n, n), 0)
    s = lax.broadcasted_iota(jnp.int32, (n, n), 1)
    return jnp.where(j > s, 1.0, 0.0).astype(BF16)


def _suffix_sum(x, t):
    hi = x.astype(BF16)
    lo = (x - hi.astype(F32)).astype(BF16)
    return _dot(hi, t) + _dot(lo, t)


def _topk_select(g, q_blk, n_trailing=0):
    lane = lax.broadcasted_iota(jnp.int32, g.shape, 1)
    selm = jnp.zeros(g.shape, F32)
    for j in range(g.shape[1]):
        gj = g[:, j:j + 1]
        beats = jnp.where(g > gj, 1.0, jnp.where((g == gj) & (lane < j), 1.0, 0.0))
        rank = jnp.sum(beats, axis=-1, keepdims=True)
        if n_trailing:
            rank = rank + jnp.where(gj < NEG_INF, float(n_trailing), 0.0)
        selm = jnp.where(lane == j, jnp.where(rank < MOBA_TOPK, 1.0, 0.0), selm)
    return jnp.where(lane < q_blk, selm, 0.0)


def _ffn_kernel(x_ref, wg_ref, wu_ref, wd_ref, g_ref, b_ref, o_ref, xb_ref, acc_ref):
    j = pl.program_id(1)

    @pl.when(j == 0)
    def _():
        xb_ref[...] = x_ref[...].astype(BF16)
        acc_ref[...] = jnp.zeros_like(acc_ref)

    xb = xb_ref[...]
    gate = _dot(xb, wg_ref[...])
    up = _dot(xb, wu_ref[...])
    act = (gate * jax.nn.sigmoid(gate)) * up
    acc_ref[...] += _dot(act.astype(BF16), wd_ref[...])

    @pl.when(j == pl.num_programs(1) - 1)
    def _():
        y = DEEPNORM_ALPHA * x_ref[...] + 0.5 * acc_ref[...]
        o_ref[...] = _layer_norm(y, g_ref[...], b_ref[...])


def _ffn(x, wg, wu, wd, g, b):
    m = x.shape[0]
    tm, tf = ROW_TILE, FF_CHUNK
    nbytes = (2 * 2 * tm * D_MODEL * 4 + tm * D_MODEL * (2 + 4)
              + 2 * 3 * D_MODEL * tf * 2 + 3 * tm * tf * 4)
    return pl.pallas_call(
        _ffn_kernel,
        grid=(m // tm, D_FF_PAD // tf),
        in_specs=[
            pl.BlockSpec((tm, D_MODEL), lambda i, j: (i, 0)),
            pl.BlockSpec((D_MODEL, tf), lambda i, j: (0, j)),
            pl.BlockSpec((D_MODEL, tf), lambda i, j: (0, j)),
            pl.BlockSpec((tf, D_MODEL), lambda i, j: (j, 0)),
            pl.BlockSpec((1, D_MODEL), lambda i, j: (0, 0)),
            pl.BlockSpec((1, D_MODEL), lambda i, j: (0, 0)),
        ],
        out_specs=pl.BlockSpec((tm, D_MODEL), lambda i, j: (i, 0)),
        out_shape=jax.ShapeDtypeStruct((m, D_MODEL), F32),
        scratch_shapes=[pltpu.VMEM((tm, D_MODEL), BF16), pltpu.VMEM((tm, D_MODEL), F32)],
        compiler_params=pltpu.CompilerParams(
            dimension_semantics=("parallel", "arbitrary"),
            vmem_limit_bytes=_vmem_limit(nbytes)),
        name="ffn",
    )(x, wg, wu, wd, g, b)


def _prep_ffn(w_in, w_down):
    pad = D_FF_PAD - D_FF
    wg = jnp.pad(w_in[:, :D_FF].astype(BF16), ((0, 0), (0, pad)))
    wu = jnp.pad(w_in[:, D_FF:].astype(BF16), ((0, 0), (0, pad)))
    wd = jnp.pad(w_down.astype(BF16), ((0, pad), (0, 0)))
    return wg, wu, wd


def _proj_kernel(x_ref, w_ref, *refs):
    out_refs, xb_ref = refs[:-1], refs[-1]
    g = pl.program_id(1)

    @pl.when(g == 0)
    def _():
        xb_ref[...] = x_ref[...].astype(BF16)

    res = _dot(xb_ref[...], w_ref[...])
    for k, o_ref in enumerate(out_refs):
        @pl.when(g == k)
        def _(o_ref=o_ref):
            for h in range(o_ref.shape[1]):
                o_ref[0, h] = res[:, h * HEAD_DIM:(h + 1) * HEAD_DIM]


def _proj(x, w, heads, group, bsz):
    m = x.shape[0]
    seq = m // bsz
    tm = min(ROW_TILE, seq)
    n_st = seq // tm
    nbytes = (2 * tm * D_MODEL * 4 + tm * D_MODEL * 2 + 2 * D_MODEL * group * 2
              + 2 * sum(heads) * HEAD_DIM * tm * 4 + tm * group * 4)
    return pl.pallas_call(
        _proj_kernel,
        grid=(m // tm, len(heads)),
        in_specs=[
            pl.BlockSpec((tm, D_MODEL), lambda i, g: (i, 0)),
            pl.BlockSpec((D_MODEL, group), lambda i, g: (0, g)),
        ],
        out_specs=[pl.BlockSpec((1, nh, tm, HEAD_DIM), lambda i, g: (i // n_st, 0, i % n_st, 0))
                   for nh in heads],
        out_shape=[jax.ShapeDtypeStruct((bsz, nh, seq, HEAD_DIM), F32) for nh in heads],
        scratch_shapes=[pltpu.VMEM((tm, D_MODEL), BF16)],
        compiler_params=pltpu.CompilerParams(
            dimension_semantics=("parallel", "arbitrary"),
            vmem_limit_bytes=_vmem_limit(nbytes)),
        name="proj",
    )(x, w)


_IN_HEADS = (H_MOBA,) * 3 + (H_SB,) * 3 + (H_MEM,)


def _prep_proj_in(w_in):
    pad = PROJ_GROUP - W_MEM
    return jnp.pad(w_in.astype(BF16), ((0, 0), (0, pad)))


def _moba_prompt_kernel(slopes_ref, q_ref, k_ref, v_ref, o_ref, kb_ref, vb_ref, kmean_ref):
    h = pl.program_id(1)
    qt = pl.program_id(2)
    blk = MOBA_BLOCK
    n_blocks = k_ref.shape[2] // blk

    @pl.when(qt == 0)
    def _():
        kb_ref[...] = k_ref[0, 0].astype(BF16)
        vb_ref[...] = v_ref[0, 0].astype(BF16)
        for j in range(n_blocks):
            kmean_ref[j:j + 1, :] = jnp.mean(k_ref[0, 0, j * blk:(j + 1) * blk, :], axis=0, keepdims=True)

    slope = slopes_ref[h]
    q = q_ref[0, 0]
    qb = q.astype(BF16)

    gate = _dot_nt(q, kmean_ref[...], precision=lax.Precision.HIGHEST)
    lane = lax.broadcasted_iota(jnp.int32, gate.shape, 1)
    gate = jnp.where(lane < qt, gate, NEG_INF)
    selm = _topk_select(gate, qt)

    row = lax.broadcasted_iota(jnp.int32, (blk, blk), 0)
    col = lax.broadcasted_iota(jnp.int32, (blk, blk), 1)
    dist_i = row - col

    own = pl.multiple_of(qt * blk, blk)
    s = _dot_nt(qb, kb_ref[pl.ds(own, blk), :]) * SCALE - slope * dist_i.astype(F32)
    s = jnp.where(col <= row, s, NEG_INF)
    m0 = jnp.max(s, axis=-1, keepdims=True)
    p = jnp.exp(s - m0)
    l0 = jnp.sum(p, axis=-1, keepdims=True)
    acc0 = _dot(p.astype(BF16), vb_ref[pl.ds(own, blk), :])

    def body(j, carry):
        m, l, acc = carry
        off = pl.multiple_of(j * blk, blk)
        dist_j = (dist_i + (qt - j) * blk).astype(F32)
        s = _dot_nt(qb, kb_ref[pl.ds(off, blk), :]) * SCALE - slope * dist_j
        sel = jnp.sum(jnp.where(lane == j, selm, 0.0), axis=-1, keepdims=True)
        s = jnp.where(sel > 0.0, s, NEG_INF)
        m_new = jnp.maximum(m, jnp.max(s, axis=-1, keepdims=True))
        a = jnp.exp(m - m_new)
        p = jnp.exp(s - m_new)
        l = a * l + jnp.sum(p, axis=-1, keepdims=True)
        acc = a * acc + _dot(p.astype(BF16), vb_ref[pl.ds(off, blk), :])
        return m_new, l, acc

    _, l, acc = lax.fori_loop(0, qt, body, (m0, l0, acc0))
    o_ref[0, 0] = _rms_norm(acc / l)


def _head_attention_call(kernel_fn, name, n_heads, extra_inputs, extra_specs, extra_scratch, q, k, v):
    bsz, _, seq, _ = q.shape
    blk = MOBA_BLOCK
    q_spec = pl.BlockSpec((1, 1, blk, HEAD_DIM), lambda b, h, t: (b, h, t, 0))
    kv_spec = pl.BlockSpec((1, 1, seq, HEAD_DIM), lambda b, h, t: (b, h, 0, 0))
    nbytes = 2 * 2 * blk * HEAD_DIM * 4 + 2 * 2 * seq * HEAD_DIM * 4 + 2 * seq * HEAD_DIM * 2 + (4 << 20)
    return pl.pallas_call(
        kernel_fn,
        grid=(bsz, n_heads, seq // blk),
        in_specs=extra_specs + [q_spec, kv_spec, kv_spec],
        out_specs=q_spec,
        out_shape=jax.ShapeDtypeStruct(q.shape, F32),
        scratch_shapes=[pltpu.VMEM((seq, HEAD_DIM), BF16), pltpu.VMEM((seq, HEAD_DIM), BF16)] + extra_scratch,
        compiler_params=pltpu.CompilerParams(
            dimension_semantics=("parallel", "parallel", "arbitrary"),
            vmem_limit_bytes=_vmem_limit(nbytes)),
        name=name,
    )(*extra_inputs, q, k, v)


def _moba_prompt(q, k, v, slopes):
    n_blocks = q.shape[2] // MOBA_BLOCK
    return _head_attention_call(
        _moba_prompt_kernel, "moba_prompt", H_MOBA, [slopes], [pl.BlockSpec(memory_space=pltpu.SMEM)],
        [pltpu.VMEM((n_blocks, HEAD_DIM), F32)], q, k, v)


def _sb_tile(qb, kt, vt, t, carry, valid):
    z = _dot_nt(qb, kt) * SCALE
    ls = _log_sigmoid(z)
    lk = ls - z
    if valid is not None:
        lk = jnp.where(valid, lk, 0.0)
    after = _suffix_sum(lk, t) + carry
    w = jnp.exp(ls + after)
    if valid is not None:
        w = jnp.where(valid, w, 0.0)
    out = _dot(w.astype(BF16), vt)
    return out, carry + jnp.sum(lk, axis=-1, keepdims=True)


def _sb_prompt_kernel(q_ref, k_ref, v_ref, o_ref, kb_ref, vb_ref):
    qt = pl.program_id(2)
    blk = q_ref.shape[2]

    @pl.when(qt == 0)
    def _():
        kb_ref[...] = k_ref[0, 0].astype(BF16)
        vb_ref[...] = v_ref[0, 0].astype(BF16)

    qb = q_ref[0, 0].astype(BF16)
    t = _suffix_matrix(blk)
    row = lax.broadcasted_iota(jnp.int32, (blk, blk), 0)
    col = lax.broadcasted_iota(jnp.int32, (blk, blk), 1)

    own = pl.multiple_of(qt * blk, blk)
    acc0, carry0 = _sb_tile(qb, kb_ref[pl.ds(own, blk), :], vb_ref[pl.ds(own, blk), :], t,
                            jnp.zeros((blk, 1), F32), col < row)

    def body(i, c):
        acc, carry = c
        off = pl.multiple_of((qt - 1 - i) * blk, blk)
        out, carry = _sb_tile(qb, kb_ref[pl.ds(off, blk), :], vb_ref[pl.ds(off, blk), :], t, carry, None)
        return acc + out, carry

    acc, _ = lax.fori_loop(0, qt, body, (acc0, carry0))
    o_ref[0, 0] = _rms_norm(acc)


def _sb_prompt(q, k, v):
    return _head_attention_call(_sb_prompt_kernel, "sb_prompt", H_SB, [], [], [], q, k, v)


def _mem_prompt_kernel(q_ref, mk_ref, mv_ref, o_ref):
    for h in range(H_MEM):
        s = _dot_nt(q_ref[0, h].astype(BF16), mk_ref[0, h].astype(BF16)) * SCALE
        p = jnp.exp(s - jnp.max(s, axis=-1, keepdims=True))
        l = jnp.sum(p, axis=-1, keepdims=True)
        o = _dot(p.astype(BF16), mv_ref[0, h].astype(BF16)) / l
        o_ref[0, h] = _rms_norm(o)


def _mem_prompt(q, mk, mv):
    bsz, _, seq, _ = q.shape
    n_mem = mk.shape[2]
    tq = ROW_TILE
    nbytes = 2 * 2 * tq * W_MEM * 4 + 2 * 2 * n_mem * W_MEM * 4 + (4 << 20)
    q_spec = pl.BlockSpec((1, H_MEM, tq, HEAD_DIM), lambda b, t: (b, 0, t, 0))
    m_spec = pl.BlockSpec((1, H_MEM, n_mem, HEAD_DIM), lambda b, t: (b, 0, 0, 0))
    return pl.pallas_call(
        _mem_prompt_kernel,
        grid=(bsz, seq // tq),
        in_specs=[q_spec, m_spec, m_spec],
        out_specs=q_spec,
        out_shape=jax.ShapeDtypeStruct(q.shape, F32),
        compiler_params=pltpu.CompilerParams(
            dimension_semantics=("parallel", "parallel"),
            vmem_limit_bytes=_vmem_limit(nbytes)),
        name="mem_prompt",
    )(q, mk, mv)


def _out_kernel(*refs, n_pieces):
    o_refs = refs[:n_pieces]
    hg_ref, w_ref, x_ref, g_ref, b_ref, out_ref = refs[n_pieces:]
    acc = DEEPNORM_ALPHA * x_ref[...]
    off = 0
    for o_ref in o_refs:
        wd = o_ref.shape[1] * HEAD_DIM
        o = jnp.concatenate([o_ref[0, h] for h in range(o_ref.shape[1])], axis=1)
        ob = (o * hg_ref[:, off:off + wd]).astype(BF16)
        acc = acc + _dot(ob, w_ref[off:off + wd, :])
        off += wd
    out_ref[...] = _layer_norm(acc, g_ref[...], b_ref[...])


def _out_proj(pieces, head_g, w_out, x, g, b):
    m = x.shape[0]
    tm = ROW_TILE
    n_st = pieces[0].shape[2] // tm
    nbytes = 2 * 3 * tm * D_MODEL * 4 + 2 * D_MODEL * D_MODEL * 2 + 2 * tm * D_MODEL * 4
    row = lambda i: (i, 0)
    const = lambda i: (0, 0)
    return pl.pallas_call(
        functools.partial(_out_kernel, n_pieces=len(pieces)),
        grid=(m // tm,),
        in_specs=[pl.BlockSpec((1, p.shape[1], tm, HEAD_DIM), lambda i: (i // n_st, 0, i % n_st, 0))
                  for p in pieces] + [
            pl.BlockSpec((1, D_MODEL), const),
            pl.BlockSpec((D_MODEL, D_MODEL), const),
            pl.BlockSpec((tm, D_MODEL), row),
            pl.BlockSpec((1, D_MODEL), const),
            pl.BlockSpec((1, D_MODEL), const),
        ],
        out_specs=pl.BlockSpec((tm, D_MODEL), row),
        out_shape=jax.ShapeDtypeStruct((m, D_MODEL), F32),
        compiler_params=pltpu.CompilerParams(
            dimension_semantics=("parallel",),
            vmem_limit_bytes=_vmem_limit(nbytes)),
        name="out_proj",
    )(*pieces, head_g, w_out, x, g, b)


def _sample_kernel(pt_ref, slopes_ref, qm_ref, kmn_ref, vmn_ref, qs_ref, ksn_ref, vsn_ref, qc_ref,
                   memk_ref, memv_ref, ckm_ref, cvm_ref, cks_ref, cvs_ref, o_ref,
                   buf_ref, sem_ref, new_ref, kmean_ref, tmat_ref, tnew_ref,
                   *, layer, n_pages, past_len):
    b = pl.program_id(0)
    nb = pl.num_programs(0)
    pools = (ckm_ref, cvm_ref, cks_ref, cvs_ref)
    blk = MOBA_BLOCK
    n_blocks = past_len // blk
    rows = SAMPLE_ROWS

    def page_copy(pool, seq, p):
        page = pt_ref[seq * n_pages + p]
        return pltpu.make_async_copy(pools[pool].at[layer, page],
                                     buf_ref.at[pool, :, pl.ds(p * PAGE_SIZE, PAGE_SIZE), :],
                                     sem_ref.at[pool])

    def start_pool(pool, seq):
        for p in range(n_pages):
            page_copy(pool, seq, p).start()

    def wait_pool(pool, seq):
        for p in range(n_pages):
            page_copy(pool, seq, p).wait()

    @pl.when(b == 0)
    def _():
        new_ref[...] = jnp.zeros_like(new_ref)
        tmat_ref[...] = _suffix_matrix(blk)
        tnew_ref[...] = _suffix_matrix(NEW_TILE)
        for pool in range(4):
            start_pool(pool, b)

    def stage_new(slot, x8):
        new_ref[slot, 0:2 * rows, :] = jnp.concatenate([x8, jnp.zeros_like(x8)], axis=0).astype(BF16)

    t_row = lax.broadcasted_iota(jnp.int32, (rows, NEW_TILE), 0)
    n_col = lax.broadcasted_iota(jnp.int32, (rows, NEW_TILE), 1)
    t_pos = lax.broadcasted_iota(jnp.int32, (rows, blk), 0)
    k_pos = lax.broadcasted_iota(jnp.int32, (rows, blk), 1)

    wait_pool(0, b)
    wait_pool(1, b)

    def moba_head(h, _):
        slope = slopes_ref[h]
        q = qm_ref[0, h]
        qb = q.astype(BF16)
        stage_new(0, kmn_ref[0, h])
        stage_new(1, vmn_ref[0, h])

        scores = []
        for j in range(n_blocks):
            kf = buf_ref[0, h, j * blk:(j + 1) * blk, :]
            kmean_ref[j:j + 1, :] = jnp.mean(kf, axis=0, keepdims=True)
            scores.append(_dot_nt(qb, kf.astype(BF16)))
        gate = _dot_nt(q, kmean_ref[...], precision=lax.Precision.HIGHEST)
        selm = _topk_select(gate, n_blocks, n_trailing=1)

        s_new = _dot_nt(qb, new_ref[0]) * SCALE - slope * (t_row - n_col).astype(F32)
        s_new = jnp.where(n_col <= t_row, s_new, NEG_INF)
        m = jnp.max(s_new, axis=-1, keepdims=True)
        for j in range(n_blocks):
            dist = (t_pos - k_pos + (past_len - j * blk)).astype(F32)
            s = scores[j] * SCALE - slope * dist
            s = jnp.where(selm[:, j:j + 1] > 0.0, s, NEG_INF)
            scores[j] = s
            m = jnp.maximum(m, jnp.max(s, axis=-1, keepdims=True))

        p = jnp.exp(s_new - m)
        l = jnp.sum(p, axis=-1, keepdims=True)
        acc = _dot(p.astype(BF16), new_ref[1])
        for j in range(n_blocks):
            p = jnp.exp(scores[j] - m)
            l = l + jnp.sum(p, axis=-1, keepdims=True)
            acc = acc + _dot(p.astype(BF16), buf_ref[1, h, j * blk:(j + 1) * blk, :].astype(BF16))
        o_ref[0, h] = _rms_norm(acc / l)
        return 0

    lax.fori_loop(0, H_MOBA, moba_head, 0)

    @pl.when(b + 1 < nb)
    def _():
        start_pool(0, b + 1)
        start_pool(1, b + 1)

    wait_pool(2, b)
    wait_pool(3, b)

    def sb_head(h, _):
        qb = qs_ref[0, h].astype(BF16)
        stage_new(2, ksn_ref[0, h])
        stage_new(3, vsn_ref[0, h])
        acc, carry = _sb_tile(qb, new_ref[2], new_ref[3], tnew_ref[...],
                              jnp.zeros((rows, 1), F32), n_col < t_row)
        for j in reversed(range(n_blocks)):
            out, carry = _sb_tile(qb, buf_ref[2, h, j * blk:(j + 1) * blk, :].astype(BF16),
                                  buf_ref[3, h, j * blk:(j + 1) * blk, :].astype(BF16),
                                  tmat_ref[...], carry, None)
            acc = acc + out
        o_ref[0, H_MOBA + h] = _rms_norm(acc)
        return 0

    lax.fori_loop(0, H_SB, sb_head, 0)

    @pl.when(b + 1 < nb)
    def _():
        start_pool(2, b + 1)
        start_pool(3, b + 1)

    n_keys = memk_ref.shape[2]
    qc = jnp.concatenate([qc_ref[0, h] for h in range(H_MEM)], axis=0).astype(BF16)
    s = _dot_nt(qc, memk_ref[0, 0].astype(BF16)) * SCALE
    row_head = lax.broadcasted_iota(jnp.int32, (H_MEM * rows, n_keys), 0) // rows
    key_head = lax.broadcasted_iota(jnp.int32, (H_MEM * rows, n_keys), 1) % H_MEM
    s = jnp.where(row_head == key_head, s, NEG_INF)
    p = jnp.exp(s - jnp.max(s, axis=-1, keepdims=True))
    l = jnp.sum(p, axis=-1, keepdims=True)
    o = _dot(p.astype(BF16), memv_ref[0, 0].astype(BF16)) / l
    for h in range(H_MEM):
        o_ref[0, H_MOBA + H_SB + h] = _rms_norm(o[h * rows:(h + 1) * rows, :])


def _sample_attention(layer, page_table, slopes, qm, kmn, vmn, qs, ksn, vsn, qc, memk, memv, pools):
    nseq, n_pages = page_table.shape
    past_len = n_pages * PAGE_SIZE
    n_keys = memk.shape[2]
    n_heads_pool = pools[0].shape[2]

    def heads_spec(nh):
        return pl.BlockSpec((1, nh, SAMPLE_ROWS, HEAD_DIM), lambda b, pt: (b, 0, 0, 0))

    mem_spec = pl.BlockSpec((1, 1, n_keys, HEAD_DIM), lambda b, pt: (layer, b, 0, 0))
    any_spec = pl.BlockSpec(memory_space=pl.ANY)
    nbytes = (4 * n_heads_pool * past_len * HEAD_DIM * 4 + 2 * 2 * n_keys * HEAD_DIM * 4 + (8 << 20))
    grid_spec = pltpu.PrefetchScalarGridSpec(
        num_scalar_prefetch=1,
        grid=(nseq,),
        in_specs=[pl.BlockSpec(memory_space=pltpu.SMEM),
                  heads_spec(H_MOBA), heads_spec(H_MOBA), heads_spec(H_MOBA),
                  heads_spec(H_SB), heads_spec(H_SB), heads_spec(H_SB), heads_spec(H_MEM),
                  mem_spec, mem_spec, any_spec, any_spec, any_spec, any_spec],
        out_specs=pl.BlockSpec((1, N_HEADS, SAMPLE_ROWS, HEAD_DIM), lambda b, pt: (b, 0, 0, 0)),
        scratch_shapes=[
            pltpu.VMEM((4, n_heads_pool, past_len, HEAD_DIM), F32),
            pltpu.SemaphoreType.DMA((4,)),
            pltpu.VMEM((4, NEW_TILE, HEAD_DIM), BF16),
            pltpu.VMEM((past_len // MOBA_BLOCK, HEAD_DIM), F32),
            pltpu.VMEM((MOBA_BLOCK, MOBA_BLOCK), BF16),
            pltpu.VMEM((NEW_TILE, NEW_TILE), BF16),
        ],
    )
    return pl.pallas_call(
        functools.partial(_sample_kernel, layer=layer, n_pages=n_pages, past_len=past_len),
        grid_spec=grid_spec,
        out_shape=jax.ShapeDtypeStruct((nseq, N_HEADS, SAMPLE_ROWS, HEAD_DIM), F32),
        compiler_params=pltpu.CompilerParams(
            dimension_semantics=("arbitrary",),
            vmem_limit_bytes=_vmem_limit(nbytes)),
        name="sample_attention",
    )(page_table.reshape(-1), slopes, qm, kmn, vmn, qs, ksn, vsn, qc, memk, memv, *pools)


def _sample_rows(x, nseq, t):
    nh = x.shape[1]
    x = x.reshape(nh, nseq, t, HEAD_DIM).transpose(1, 0, 2, 3)
    return jnp.pad(x, ((0, 0), (0, 0), (0, SAMPLE_ROWS - t), (0, 0)))


def kernel(x_prompt, x_sample, cache_moba_k, cache_moba_v, cache_sb_k, cache_sb_v, cache_mem_k, cache_mem_v, page_table, mem_prompt, ln1_g, ln1_b, ln2_g, ln2_b, ln3_g, ln3_b, ffn1_w_in, ffn1_w_down, w_in, w_mem_kv, head_g, w_out, ffn2_w_in, ffn2_w_down):
    bsz, seq, _ = x_prompt.shape
    nseq, t_new, _ = x_sample.shape
    n_mem = mem_prompt.shape[1]
    assert seq % MOBA_BLOCK == 0 and seq % ROW_TILE == 0 and t_new <= SAMPLE_ROWS // 2
    assert (page_table.shape[1] * PAGE_SIZE) % MOBA_BLOCK == 0 and (nseq * t_new) % ROW_TILE == 0

    slopes = jnp.exp2(-8.0 * jnp.arange(1, H_MOBA + 1, dtype=F32) / H_MOBA)

    xp = x_prompt.reshape(bsz * seq, D_MODEL)
    xs = x_sample.reshape(nseq * t_new, D_MODEL)
    mem2d = mem_prompt.reshape(bsz * n_mem, D_MODEL)
    pools = [c.transpose(0, 1, 3, 2, 4) for c in (cache_moba_k, cache_moba_v, cache_sb_k, cache_sb_v)]
    memk = cache_mem_k.reshape(DEPTH, nseq, n_mem * H_MEM, HEAD_DIM)
    memv = cache_mem_v.reshape(DEPTH, nseq, n_mem * H_MEM, HEAD_DIM)
    row = lambda v: v.reshape(1, -1)

    leaves = [[] for _ in range(10)]
    for l in range(DEPTH):
        f1 = _prep_ffn(ffn1_w_in[l], ffn1_w_down[l])
        f2 = _prep_ffn(ffn2_w_in[l], ffn2_w_down[l])
        w_in_b = _prep_proj_in(w_in[l])
        w_mem_b = w_mem_kv[l].astype(BF16)
        w_out_b = w_out[l].astype(BF16)
        hg = row(head_g[l])

        xp = _ffn(xp, *f1, row(ln1_g[l]), row(ln1_b[l]))
        xs = _ffn(xs, *f1, row(ln1_g[l]), row(ln1_b[l]))

        qm, km, vm, qs, ks, vs, qc = _proj(xp, w_in_b, _IN_HEADS, PROJ_GROUP, bsz)
        mk, mv = _proj(mem2d, w_mem_b, (H_MEM, H_MEM), W_MEM, bsz)
        o_moba = _moba_prompt(qm, km, vm, slopes)
        o_sb = _sb_prompt(qs, ks, vs)
        o_mem = _mem_prompt(qc, mk, mv)
        xp = _out_proj([o_moba, o_sb, o_mem], hg, w_out_b, xp, row(ln2_g[l]), row(ln2_b[l]))

        qm2, km2, vm2, qs2, ks2, vs2, qc2 = _proj(xs, w_in_b, _IN_HEADS, PROJ_GROUP, 1)
        pad = lambda a: _sample_rows(a, nseq, t_new)
        o8 = _sample_attention(l, page_table, slopes, pad(qm2), pad(km2), pad(vm2),
                               pad(qs2), pad(ks2), pad(vs2), pad(qc2), memk, memv, pools)
        o2 = o8[:, :, :t_new, :].transpose(1, 0, 2, 3).reshape(1, N_HEADS, nseq * t_new, HEAD_DIM)
        xs = _out_proj([o2], hg, w_out_b, xs, row(ln2_g[l]), row(ln2_b[l]))

        xp = _ffn(xp, *f2, row(ln3_g[l]), row(ln3_b[l]))
        xs = _ffn(xs, *f2, row(ln3_g[l]), row(ln3_b[l]))

        for dst, val in zip(leaves, (km, vm, ks, vs, mk, mv, km2, vm2, ks2, vs2)):
            dst.append(val)

    def prompt_leaf(vals):
        return jnp.stack(vals).transpose(0, 1, 3, 2, 4)

    def sample_leaf(vals):
        x = jnp.stack(vals)
        return x.reshape(DEPTH, x.shape[2], nseq, t_new, HEAD_DIM).transpose(0, 2, 3, 1, 4)

    return (
        xp.reshape(bsz, seq, D_MODEL), xs.reshape(nseq, t_new, D_MODEL),
        prompt_leaf(leaves[0]), prompt_leaf(leaves[1]), prompt_leaf(leaves[2]), prompt_leaf(leaves[3]),
        prompt_leaf(leaves[4]), prompt_leaf(leaves[5]),
        sample_leaf(leaves[6]), sample_leaf(leaves[7]), sample_leaf(leaves[8]), sample_leaf(leaves[9]),
    )
```

```python
import functools

import jax
import jax.numpy as jnp
from jax import lax
from jax.experimental import pallas as pl
from jax.experimental.pallas import tpu as pltpu

F32 = jnp.float32
BF16 = jnp.bfloat16

D_MODEL = 2048
DEPTH = 2
HEAD_DIM = 128
H_MOBA = 6
H_SB = 6
H_MEM = 4
N_HEADS = H_MOBA + H_SB + H_MEM
W_MOBA = H_MOBA * HEAD_DIM
W_SB = H_SB * HEAD_DIM
W_MEM = H_MEM * HEAD_DIM
MOBA_BLOCK = 256
MOBA_TOPK = 3
PAGE_SIZE = 128
D_FF = 5504
DEEPNORM_ALPHA = (2.0 * DEPTH) ** 0.25
LN_EPS = 1e-5
RMS_EPS = 1e-6
NEG_INF = -1e30
SCALE = HEAD_DIM ** -0.5

LANES_V7X = 128
SUBLANES_V7X = 8
VMEM_BYTES_V7X = 64 * 1024 * 1024

FF_CHUNK = 512
D_FF_PAD = -(-D_FF // FF_CHUNK) * FF_CHUNK
ROW_TILE = 512
PROJ_GROUP = W_MOBA
SAMPLE_ROWS = SUBLANES_V7X

def _vmem_limit(nbytes):
    return int(min(nbytes + (8 << 20), VMEM_BYTES_V7X - (6 << 20)))


def _dot(a, b):
    return jnp.dot(a, b, preferred_element_type=F32)


def _dot_nt(a, b, precision=None):
    return lax.dot_general(a, b, (((1,), (1,)), ((), ())),
                           preferred_element_type=F32, precision=precision)


def _layer_norm(y, g, b):
    mu = jnp.mean(y, axis=-1, keepdims=True)
    d = y - mu
    var = jnp.mean(d * d, axis=-1, keepdims=True)
    return d * lax.rsqrt(var + LN_EPS) * g + b


def _rms_norm(o):
    return o * lax.rsqrt(jnp.mean(o * o, axis=-1, keepdims=True) + RMS_EPS)


def _log_sigmoid(z):
    return jnp.minimum(z, 0.0) - jnp.log1p(jnp.exp(-jnp.abs(z)))


def _suffix_matrix(n, keys_on_rows=False):
    r = lax.broadcasted_iota(jnp.int32, (n, n), 0)
    c = lax.broadcasted_iota(jnp.int32, (n, n), 1)
    return jnp.where((c > r) if keys_on_rows else (r > c), 1.0, 0.0).astype(BF16)


def _split_bf16(x):
    hi = x.astype(BF16)
    lo = (x - hi.astype(F32)).astype(BF16)
    return hi, lo


def _suffix_sum(x, t):
    hi, lo = _split_bf16(x)
    return _dot(hi, t) + _dot(lo, t)


def _topk_select(g, q_blk, axis, n_trailing=0):
    idx = lax.broadcasted_iota(jnp.int32, g.shape, axis)
    selm = jnp.zeros(g.shape, F32)
    for j in range(g.shape[axis]):
        gj = lax.slice_in_dim(g, j, j + 1, axis=axis)
        beats = jnp.where(g > gj, 1.0, jnp.where((g == gj) & (idx < j), 1.0, 0.0))
        rank = jnp.sum(beats, axis=axis, keepdims=True)
        if n_trailing:
            rank = rank + jnp.where(gj < NEG_INF, float(n_trailing), 0.0)
        selm = jnp.where(idx == j, jnp.where(rank < MOBA_TOPK, 1.0, 0.0), selm)
    return jnp.where(idx < q_blk, selm, 0.0)


def _ffn_kernel(x_ref, wg_ref, wu_ref, wd_ref, g_ref, b_ref, o_ref, xb_ref, acc_ref):
    j = pl.program_id(1)

    @pl.when(j == 0)
    def _():
        xb_ref[...] = x_ref[...].astype(BF16)
        acc_ref[...] = jnp.zeros_like(acc_ref)

    xb = xb_ref[...]
    gate = _dot(xb, wg_ref[...])
    up = _dot(xb, wu_ref[...])
    act = (gate * jax.nn.sigmoid(gate)) * up
    acc_ref[...] += _dot(act.astype(BF16), wd_ref[...])

    @pl.when(j == pl.num_programs(1) - 1)
    def _():
        y = DEEPNORM_ALPHA * x_ref[...] + 0.5 * acc_ref[...]
        o_ref[...] = _layer_norm(y, g_ref[...], b_ref[...])


def _ffn(x, wg, wu, wd, g, b):
    m = x.shape[0]
    tm, tf = ROW_TILE, FF_CHUNK
    nbytes = (2 * 2 * tm * D_MODEL * 4 + tm * D_MODEL * (2 + 4)
              + 2 * 3 * D_MODEL * tf * 2 + 3 * tm * tf * 4)
    return pl.pallas_call(
        _ffn_kernel,
        grid=(m // tm, D_FF_PAD // tf),
        in_specs=[
            pl.BlockSpec((tm, D_MODEL), lambda i, j: (i, 0)),
            pl.BlockSpec((D_MODEL, tf), lambda i, j: (0, j)),
            pl.BlockSpec((D_MODEL, tf), lambda i, j: (0, j)),
            pl.BlockSpec((tf, D_MODEL), lambda i, j: (j, 0)),
            pl.BlockSpec((1, D_MODEL), lambda i, j: (0, 0)),
            pl.BlockSpec((1, D_MODEL), lambda i, j: (0, 0)),
        ],
        out_specs=pl.BlockSpec((tm, D_MODEL), lambda i, j: (i, 0)),
        out_shape=jax.ShapeDtypeStruct((m, D_MODEL), F32),
        scratch_shapes=[pltpu.VMEM((tm, D_MODEL), BF16), pltpu.VMEM((tm, D_MODEL), F32)],
        compiler_params=pltpu.CompilerParams(
            dimension_semantics=("parallel", "arbitrary"),
            vmem_limit_bytes=_vmem_limit(nbytes)),
        name="ffn",
    )(x, wg, wu, wd, g, b)


def _prep_ffn(w_in, w_down):
    pad = D_FF_PAD - D_FF
    wg = jnp.pad(w_in[:, :D_FF].astype(BF16), ((0, 0), (0, pad)))
    wu = jnp.pad(w_in[:, D_FF:].astype(BF16), ((0, 0), (0, pad)))
    wd = jnp.pad(w_down.astype(BF16), ((0, pad), (0, 0)))
    return wg, wu, wd


def _proj_kernel(x_ref, w_ref, *refs):
    out_refs, xb_ref = refs[:-1], refs[-1]
    g = pl.program_id(1)

    @pl.when(g == 0)
    def _():
        xb_ref[...] = x_ref[...].astype(BF16)

    res = _dot(xb_ref[...], w_ref[...])
    for k, o_ref in enumerate(out_refs):
        @pl.when(g == k)
        def _(o_ref=o_ref):
            for h in range(o_ref.shape[1]):
                o_ref[0, h] = res[:, h * HEAD_DIM:(h + 1) * HEAD_DIM]


def _proj(x, w, heads, group, bsz):
    m = x.shape[0]
    seq = m // bsz
    tm = min(ROW_TILE, seq)
    n_st = seq // tm
    nbytes = (2 * tm * D_MODEL * 4 + tm * D_MODEL * 2 + 2 * D_MODEL * group * 2
              + 2 * sum(heads) * HEAD_DIM * tm * 4 + tm * group * 4)
    return pl.pallas_call(
        _proj_kernel,
        grid=(m // tm, len(heads)),
        in_specs=[
            pl.BlockSpec((tm, D_MODEL), lambda i, g: (i, 0)),
            pl.BlockSpec((D_MODEL, group), lambda i, g: (0, g)),
        ],
        out_specs=[pl.BlockSpec((1, nh, tm, HEAD_DIM), lambda i, g: (i // n_st, 0, i % n_st, 0))
                   for nh in heads],
        out_shape=[jax.ShapeDtypeStruct((bsz, nh, seq, HEAD_DIM), F32) for nh in heads],
        scratch_shapes=[pltpu.VMEM((tm, D_MODEL), BF16)],
        compiler_params=pltpu.CompilerParams(
            dimension_semantics=("parallel", "arbitrary"),
            vmem_limit_bytes=_vmem_limit(nbytes)),
        name="proj",
    )(x, w)


_IN_HEADS = (H_MOBA,) * 3 + (H_SB,) * 3 + (H_MEM,)


def _prep_proj_in(w_in):
    pad = PROJ_GROUP - W_MEM
    return jnp.pad(w_in.astype(BF16), ((0, 0), (0, pad)))


def _stage_kv(k_ref, v_ref, kb_ref, vt_ref):
    blk = MOBA_BLOCK
    kb_ref[...] = k_ref[0, 0].astype(BF16)
    for j in range(k_ref.shape[2] // blk):
        vt_ref[:, j * blk:(j + 1) * blk] = v_ref[0, 0, j * blk:(j + 1) * blk, :].T.astype(BF16)


def _rms_norm_cols(o):
    return o * lax.rsqrt(jnp.mean(o * o, axis=0, keepdims=True) + RMS_EPS)


def _software_pipeline(items, stages):
    n, depth = len(items), len(stages)
    results = {}
    for step in range(n + depth - 1):
        for k, stage in enumerate(stages):
            i = step - k
            if 0 <= i < n:
                results[i] = stage(items[i], results.get(i))


def _tile(t):
    return slice(t * MOBA_BLOCK, (t + 1) * MOBA_BLOCK)


def _moba_prompt_kernel(slopes_ref, q_ref, k_ref, v_ref, o_ref, kb_ref, vt_ref, qb_ref, kmean_ref):
    h = pl.program_id(1)
    blk = MOBA_BLOCK
    n_blocks = k_ref.shape[2] // blk

    _stage_kv(k_ref, v_ref, kb_ref, vt_ref)
    qb_ref[...] = q_ref[0, 0].astype(BF16)
    for j in range(n_blocks):
        kmean_ref[j:j + 1, :] = jnp.mean(k_ref[0, 0, _tile(j), :], axis=0, keepdims=True)

    slope = slopes_ref[h]
    gate = _dot_nt(kmean_ref[...], q_ref[0, 0], precision=lax.Precision.HIGHEST)
    jblk = lax.broadcasted_iota(jnp.int32, (n_blocks, blk), 0)
    key = lax.broadcasted_iota(jnp.int32, (blk, blk), 0)
    qi = lax.broadcasted_iota(jnp.int32, (blk, blk), 1)
    dist_own = (qi - key).astype(F32)
    selm, stats, accs = {}, {}, {}

    def scores(item, _):
        qt, j = item
        return _dot_nt(kb_ref[_tile(j), :], qb_ref[_tile(qt), :])

    def softmax(item, s):
        qt, j = item
        if j == qt:
            if qt > 0:
                selm[qt] = _topk_select(jnp.where(jblk < qt, gate[:, _tile(qt)], NEG_INF), qt, axis=0)
            s = s * SCALE - slope * dist_own
            s = jnp.where(key <= qi, s, NEG_INF)
            m = jnp.max(s, axis=0, keepdims=True)
            a = None
            p = jnp.exp(s - m)
            l = jnp.sum(p, axis=0, keepdims=True)
        else:
            m_old, l_old = stats[qt]
            s = s * SCALE - slope * (dist_own + float((qt - j) * blk))
            s = jnp.where(selm[qt][j:j + 1, :] > 0.0, s, NEG_INF)
            m = jnp.maximum(m_old, jnp.max(s, axis=0, keepdims=True))
            a = jnp.exp(m_old - m)
            p = jnp.exp(s - m)
            l = a * l_old + jnp.sum(p, axis=0, keepdims=True)
        stats[qt] = (m, l)
        return a, l, _dot(vt_ref[:, _tile(j)], p.astype(BF16))

    def accumulate(item, res):
        qt, j = item
        a, l, pv = res
        accs[qt] = pv if a is None else a * accs[qt] + pv
        if j == qt - 1 or qt == 0:
            o_ref[0, 0, _tile(qt), :] = _rms_norm_cols(accs.pop(qt) / l).T

    items = [(qt, j) for qt in range(n_blocks) for j in [qt] + list(range(qt))]
    _software_pipeline(items, [scores, softmax, accumulate])


def _head_attention_call(kernel_fn, name, n_heads, extra_inputs, extra_specs, extra_scratch, q, k, v):
    bsz, _, seq, _ = q.shape
    spec = pl.BlockSpec((1, 1, seq, HEAD_DIM), lambda b, h: (b, h, 0, 0))
    nbytes = 4 * 2 * seq * HEAD_DIM * 4 + 3 * seq * HEAD_DIM * 2 + (16 << 20)
    return pl.pallas_call(
        kernel_fn,
        grid=(bsz, n_heads),
        in_specs=extra_specs + [spec, spec, spec],
        out_specs=spec,
        out_shape=jax.ShapeDtypeStruct(q.shape, F32),
        scratch_shapes=[pltpu.VMEM((seq, HEAD_DIM), BF16), pltpu.VMEM((HEAD_DIM, seq), BF16),
                        pltpu.VMEM((seq, HEAD_DIM), BF16)] + extra_scratch,
        compiler_params=pltpu.CompilerParams(
            dimension_semantics=("parallel", "parallel"),
            vmem_limit_bytes=_vmem_limit(nbytes)),
        name=name,
    )(*extra_inputs, q, k, v)


def _moba_prompt(q, k, v, slopes):
    n_blocks = q.shape[2] // MOBA_BLOCK
    return _head_attention_call(
        _moba_prompt_kernel, "moba_prompt", H_MOBA, [slopes], [pl.BlockSpec(memory_space=pltpu.SMEM)],
        [pltpu.VMEM((n_blocks, HEAD_DIM), F32)], q, k, v)


def _sb_prompt_kernel(q_ref, k_ref, v_ref, o_ref, kb_ref, vt_ref, qb_ref, t_ref):
    blk = MOBA_BLOCK
    n_tiles = k_ref.shape[2] // blk

    _stage_kv(k_ref, v_ref, kb_ref, vt_ref)
    qb_ref[...] = q_ref[0, 0].astype(BF16)
    t_ref[...] = _suffix_matrix(blk, keys_on_rows=True)
    key = lax.broadcasted_iota(jnp.int32, (blk, blk), 0)
    qi = lax.broadcasted_iota(jnp.int32, (blk, blk), 1)
    own_valid = key < qi
    carries, accs = {}, {}

    def scores(item, _):
        qt, j = item
        return _dot_nt(kb_ref[_tile(j), :], qb_ref[_tile(qt), :])

    def suffix_sums(item, z):
        qt, j = item
        z = z * SCALE
        ls = _log_sigmoid(z)
        lk = ls - z
        if j == qt:
            lk = jnp.where(own_valid, lk, 0.0)
        hi, lo = _split_bf16(lk)
        carry = carries.get(qt)
        total = jnp.sum(lk, axis=0, keepdims=True)
        carries[qt] = total if carry is None else carry + total
        return ls, carry, _dot(t_ref[...], hi) + _dot(t_ref[...], lo)

    def weigh_values(item, res):
        qt, j = item
        ls, carry, after = res
        if carry is not None:
            after = after + carry
        w = jnp.exp(ls + after)
        if j == qt:
            w = jnp.where(own_valid, w, 0.0)
        return _dot(vt_ref[:, _tile(j)], w.astype(BF16))

    def accumulate(item, out):
        qt, j = item
        accs[qt] = out if j == qt else accs[qt] + out
        if j == 0:
            o_ref[0, 0, _tile(qt), :] = _rms_norm_cols(accs.pop(qt)).T

    items = [(qt, j) for qt in range(n_tiles) for j in range(qt, -1, -1)]
    _software_pipeline(items, [scores, suffix_sums, weigh_values, accumulate])


def _sb_prompt(q, k, v):
    return _head_attention_call(_sb_prompt_kernel, "sb_prompt", H_SB, [], [],
                                [pltpu.VMEM((MOBA_BLOCK, MOBA_BLOCK), BF16)], q, k, v)


def _mem_prompt_kernel(q_ref, mk_ref, mv_ref, o_ref):
    for h in range(H_MEM):
        s = _dot_nt(q_ref[0, h].astype(BF16), mk_ref[0, h].astype(BF16)) * SCALE
        p = jnp.exp(s - jnp.max(s, axis=-1, keepdims=True))
        l = jnp.sum(p, axis=-1, keepdims=True)
        o = _dot(p.astype(BF16), mv_ref[0, h].astype(BF16)) / l
        o_ref[0, h] = _rms_norm(o)


def _mem_prompt(q, mk, mv):
    bsz, _, seq, _ = q.shape
    n_mem = mk.shape[2]
    tq = ROW_TILE
    nbytes = 2 * 2 * tq * W_MEM * 4 + 2 * 2 * n_mem * W_MEM * 4 + (4 << 20)
    q_spec = pl.BlockSpec((1, H_MEM, tq, HEAD_DIM), lambda b, t: (b, 0, t, 0))
    m_spec = pl.BlockSpec((1, H_MEM, n_mem, HEAD_DIM), lambda b, t: (b, 0, 0, 0))
    return pl.pallas_call(
        _mem_prompt_kernel,
        grid=(bsz, seq // tq),
        in_specs=[q_spec, m_spec, m_spec],
        out_specs=q_spec,
        out_shape=jax.ShapeDtypeStruct(q.shape, F32),
        compiler_params=pltpu.CompilerParams(
            dimension_semantics=("parallel", "parallel"),
            vmem_limit_bytes=_vmem_limit(nbytes)),
        name="mem_prompt",
    )(q, mk, mv)


def _out_kernel(*refs, n_pieces):
    o_refs = refs[:n_pieces]
    hg_ref, w_ref, x_ref, g_ref, b_ref, out_ref = refs[n_pieces:]
    acc = DEEPNORM_ALPHA * x_ref[...]
    off = 0
    for o_ref in o_refs:
        wd = o_ref.shape[1] * HEAD_DIM
        o = jnp.concatenate([o_ref[0, h] for h in range(o_ref.shape[1])], axis=1)
        ob = (o * hg_ref[:, off:off + wd]).astype(BF16)
        acc = acc + _dot(ob, w_ref[off:off + wd, :])
        off += wd
    out_ref[...] = _layer_norm(acc, g_ref[...], b_ref[...])


def _out_proj(pieces, head_g, w_out, x, g, b):
    m = x.shape[0]
    tm = ROW_TILE
    n_st = pieces[0].shape[2] // tm
    nbytes = 2 * 3 * tm * D_MODEL * 4 + 2 * D_MODEL * D_MODEL * 2 + 2 * tm * D_MODEL * 4
    row = lambda i: (i, 0)
    const = lambda i: (0, 0)
    return pl.pallas_call(
        functools.partial(_out_kernel, n_pieces=len(pieces)),
        grid=(m // tm,),
        in_specs=[pl.BlockSpec((1, p.shape[1], tm, HEAD_DIM), lambda i: (i // n_st, 0, i % n_st, 0))
                  for p in pieces] + [
            pl.BlockSpec((1, D_MODEL), const),
            pl.BlockSpec((D_MODEL, D_MODEL), const),
            pl.BlockSpec((tm, D_MODEL), row),
            pl.BlockSpec((1, D_MODEL), const),
            pl.BlockSpec((1, D_MODEL), const),
        ],
        out_specs=pl.BlockSpec((tm, D_MODEL), row),
        out_shape=jax.ShapeDtypeStruct((m, D_MODEL), F32),
        compiler_params=pltpu.CompilerParams(
            dimension_semantics=("parallel",),
            vmem_limit_bytes=_vmem_limit(nbytes)),
        name="out_proj",
    )(*pieces, head_g, w_out, x, g, b)


def _sample_kernel(pt_ref, slope_ref, qm_ref, kmn_ref, vmn_ref, qs_ref, ksn_ref, vsn_ref, qc_ref,
                   memk_ref, memv_ref, ckm_ref, cvm_ref, cks_ref, cvs_ref, o_ref,
                   buf_ref, sem_ref, s_ref, kmean_ref, tmat_ref,
                   *, layer, n_pages, past_len):
    b = pl.program_id(0)
    nb = pl.num_programs(0)
    pools = (ckm_ref, cvm_ref, cks_ref, cvs_ref)
    blk = MOBA_BLOCK
    n_blocks = past_len // blk
    n_tiles = n_blocks + 1
    rows = SAMPLE_ROWS
    n_rows = s_ref.shape[0]

    def page_copy(pool, seq, p):
        page = pt_ref[seq * n_pages + p]
        return pltpu.make_async_copy(pools[pool].at[layer, page],
                                     buf_ref.at[pool, :, pl.ds(p * PAGE_SIZE, PAGE_SIZE), :],
                                     sem_ref.at[pool])

    def start_pool(pool, seq):
        for p in range(n_pages):
            page_copy(pool, seq, p).start()

    def wait_pool(pool, seq):
        for p in range(n_pages):
            page_copy(pool, seq, p).wait()

    def start_next(pool):
        @pl.when(b + 1 < nb)
        def _():
            start_pool(pool, b + 1)

    def head_rows(h):
        return slice(h * rows, (h + 1) * rows)

    def tile_cols(j):
        return slice(j * blk, (j + 1) * blk)

    @pl.when(b == 0)
    def _():
        buf_ref[:, :, past_len:, :] = jnp.zeros((4, buf_ref.shape[1], blk, HEAD_DIM), F32)
        tmat_ref[...] = _suffix_matrix(blk)
        for pool in range(4):
            start_pool(pool, b)

    for pool, new in enumerate((kmn_ref, vmn_ref, ksn_ref, vsn_ref)):
        buf_ref[pool, :, past_len:past_len + rows, :] = new[0]

    t_q = lax.broadcasted_iota(jnp.int32, (n_rows, blk), 0) % rows
    k_in = lax.broadcasted_iota(jnp.int32, (n_rows, blk), 1)
    slope = slope_ref[...]

    def weighted_values(pool):
        for h in range(buf_ref.shape[1]):
            w = s_ref[head_rows(h), :].astype(BF16)
            yield h, _dot(w, buf_ref[pool, h].astype(BF16))

    wait_pool(0, b)
    gates = []
    for h in range(H_MOBA):
        q = qm_ref[0, h]
        for j in range(n_blocks):
            kmean_ref[h * n_blocks + j:h * n_blocks + j + 1, :] = jnp.mean(
                buf_ref[0, h, tile_cols(j), :], axis=0, keepdims=True)
        s_ref[head_rows(h), :] = _dot_nt(q.astype(BF16), buf_ref[0, h].astype(BF16))
        gates.append(_dot_nt(q, kmean_ref[h * n_blocks:(h + 1) * n_blocks, :],
                             precision=lax.Precision.HIGHEST))
    start_next(0)
    selm = _topk_select(jnp.concatenate(gates, axis=0), n_blocks, axis=1, n_trailing=1)

    tiles = []
    for j in range(n_tiles):
        dist = (t_q - k_in + (past_len - j * blk)).astype(F32)
        s = s_ref[:, tile_cols(j)] * SCALE - slope * dist
        if j < n_blocks:
            s = jnp.where(selm[:, j:j + 1] > 0.0, s, NEG_INF)
        else:
            s = jnp.where(k_in <= t_q, s, NEG_INF)
        tiles.append(s)
    m = functools.reduce(jnp.maximum, [jnp.max(s, axis=-1, keepdims=True) for s in tiles])
    l = jnp.zeros((n_rows, 1), F32)
    for j in range(n_tiles):
        p = jnp.exp(tiles[j] - m)
        l = l + jnp.sum(p, axis=-1, keepdims=True)
        s_ref[:, tile_cols(j)] = p
    wait_pool(1, b)
    for h, out in weighted_values(1):
        o_ref[0, h] = _rms_norm(out / l[head_rows(h), :])
    start_next(1)

    wait_pool(2, b)
    for h in range(H_SB):
        s_ref[head_rows(h), :] = _dot_nt(qs_ref[0, h].astype(BF16), buf_ref[2, h].astype(BF16))
    start_next(2)
    new_valid = k_in < t_q
    log_beta, log_keep = [], []
    for j in range(n_tiles):
        z = s_ref[:, tile_cols(j)] * SCALE
        ls = _log_sigmoid(z)
        lk = ls - z
        if j == n_blocks:
            lk = jnp.where(new_valid, lk, 0.0)
        log_beta.append(ls)
        log_keep.append(lk)
    after = _suffix_sum(jnp.concatenate(log_keep, axis=0), tmat_ref[...])
    carry = jnp.zeros((n_rows, 1), F32)
    for j in reversed(range(n_tiles)):
        w = jnp.exp(log_beta[j] + after[j * n_rows:(j + 1) * n_rows, :] + carry)
        if j == n_blocks:
            w = jnp.where(new_valid, w, 0.0)
        s_ref[:, tile_cols(j)] = w
        carry = carry + jnp.sum(log_keep[j], axis=-1, keepdims=True)
    wait_pool(3, b)
    for h, out in weighted_values(3):
        o_ref[0, H_MOBA + h] = _rms_norm(out)
    start_next(3)

    n_keys = memk_ref.shape[2]
    qc = jnp.concatenate([qc_ref[0, h] for h in range(H_MEM)], axis=0).astype(BF16)
    s = _dot_nt(qc, memk_ref[0, 0].astype(BF16)) * SCALE
    row_head = lax.broadcasted_iota(jnp.int32, (H_MEM * rows, n_keys), 0) // rows
    key_head = lax.broadcasted_iota(jnp.int32, (H_MEM * rows, n_keys), 1) % H_MEM
    s = jnp.where(row_head == key_head, s, NEG_INF)
    p = jnp.exp(s - jnp.max(s, axis=-1, keepdims=True))
    l = jnp.sum(p, axis=-1, keepdims=True)
    o = _dot(p.astype(BF16), memv_ref[0, 0].astype(BF16)) / l
    for h in range(H_MEM):
        o_ref[0, H_MOBA + H_SB + h] = _rms_norm(o[h * rows:(h + 1) * rows, :])


def _sample_attention(layer, page_table, slope_rows, qm, kmn, vmn, qs, ksn, vsn, qc, memk, memv, pools):
    nseq, n_pages = page_table.shape
    past_len = n_pages * PAGE_SIZE
    n_keys = memk.shape[2]
    n_heads_pool = pools[0].shape[2]

    def heads_spec(nh):
        return pl.BlockSpec((1, nh, SAMPLE_ROWS, HEAD_DIM), lambda b, pt: (b, 0, 0, 0))

    mem_spec = pl.BlockSpec((1, 1, n_keys, HEAD_DIM), lambda b, pt: (layer, b, 0, 0))
    any_spec = pl.BlockSpec(memory_space=pl.ANY)
    key_len = past_len + MOBA_BLOCK
    n_rows = n_heads_pool * SAMPLE_ROWS
    n_blocks = past_len // MOBA_BLOCK
    nbytes = (4 * n_heads_pool * key_len * HEAD_DIM * 4 + 2 * 2 * n_keys * HEAD_DIM * 4
              + n_rows * key_len * 4 + (8 << 20))
    grid_spec = pltpu.PrefetchScalarGridSpec(
        num_scalar_prefetch=1,
        grid=(nseq,),
        in_specs=[pl.BlockSpec((n_rows, 1), lambda b, pt: (0, 0)),
                  heads_spec(H_MOBA), heads_spec(H_MOBA), heads_spec(H_MOBA),
                  heads_spec(H_SB), heads_spec(H_SB), heads_spec(H_SB), heads_spec(H_MEM),
                  mem_spec, mem_spec, any_spec, any_spec, any_spec, any_spec],
        out_specs=pl.BlockSpec((1, N_HEADS, SAMPLE_ROWS, HEAD_DIM), lambda b, pt: (b, 0, 0, 0)),
        scratch_shapes=[
            pltpu.VMEM((4, n_heads_pool, key_len, HEAD_DIM), F32),
            pltpu.SemaphoreType.DMA((4,)),
            pltpu.VMEM((n_rows, key_len), F32),
            pltpu.VMEM((n_heads_pool * n_blocks, HEAD_DIM), F32),
            pltpu.VMEM((MOBA_BLOCK, MOBA_BLOCK), BF16),
        ],
    )
    return pl.pallas_call(
        functools.partial(_sample_kernel, layer=layer, n_pages=n_pages, past_len=past_len),
        grid_spec=grid_spec,
        out_shape=jax.ShapeDtypeStruct((nseq, N_HEADS, SAMPLE_ROWS, HEAD_DIM), F32),
        compiler_params=pltpu.CompilerParams(
            dimension_semantics=("arbitrary",),
            vmem_limit_bytes=_vmem_limit(nbytes)),
        name="sample_attention",
    )(page_table.reshape(-1), slope_rows, qm, kmn, vmn, qs, ksn, vsn, qc, memk, memv, *pools)


def _sample_rows(x, nseq, t):
    nh = x.shape[1]
    x = x.reshape(nh, nseq, t, HEAD_DIM).transpose(1, 0, 2, 3)
    return jnp.pad(x, ((0, 0), (0, 0), (0, SAMPLE_ROWS - t), (0, 0)))


def kernel(x_prompt, x_sample, cache_moba_k, cache_moba_v, cache_sb_k, cache_sb_v, cache_mem_k, cache_mem_v, page_table, mem_prompt, ln1_g, ln1_b, ln2_g, ln2_b, ln3_g, ln3_b, ffn1_w_in, ffn1_w_down, w_in, w_mem_kv, head_g, w_out, ffn2_w_in, ffn2_w_down):
    bsz, seq, _ = x_prompt.shape
    nseq, t_new, _ = x_sample.shape
    n_mem = mem_prompt.shape[1]
    assert seq % MOBA_BLOCK == 0 and seq % ROW_TILE == 0 and t_new <= SAMPLE_ROWS // 2
    assert (page_table.shape[1] * PAGE_SIZE) % MOBA_BLOCK == 0 and (nseq * t_new) % ROW_TILE == 0

    slopes = jnp.exp2(-8.0 * jnp.arange(1, H_MOBA + 1, dtype=F32) / H_MOBA)
    slope_rows = jnp.repeat(slopes, SAMPLE_ROWS).reshape(H_MOBA * SAMPLE_ROWS, 1)

    xp = x_prompt.reshape(bsz * seq, D_MODEL)
    xs = x_sample.reshape(nseq * t_new, D_MODEL)
    mem2d = mem_prompt.reshape(bsz * n_mem, D_MODEL)
    pools = [c.transpose(0, 1, 3, 2, 4) for c in (cache_moba_k, cache_moba_v, cache_sb_k, cache_sb_v)]
    memk = cache_mem_k.reshape(DEPTH, nseq, n_mem * H_MEM, HEAD_DIM)
    memv = cache_mem_v.reshape(DEPTH, nseq, n_mem * H_MEM, HEAD_DIM)
    row = lambda v: v.reshape(1, -1)

    leaves = [[] for _ in range(10)]
    for l in range(DEPTH):
        f1 = _prep_ffn(ffn1_w_in[l], ffn1_w_down[l])
        f2 = _prep_ffn(ffn2_w_in[l], ffn2_w_down[l])
        w_in_b = _prep_proj_in(w_in[l])
        w_mem_b = w_mem_kv[l].astype(BF16)
        w_out_b = w_out[l].astype(BF16)
        hg = row(head_g[l])

        xp = _ffn(xp, *f1, row(ln1_g[l]), row(ln1_b[l]))
        xs = _ffn(xs, *f1, row(ln1_g[l]), row(ln1_b[l]))

        qm, km, vm, qs, ks, vs, qc = _proj(xp, w_in_b, _IN_HEADS, PROJ_GROUP, bsz)
        mk, mv = _proj(mem2d, w_mem_b, (H_MEM, H_MEM), W_MEM, bsz)
        o_moba = _moba_prompt(qm, km, vm, slopes)
        o_sb = _sb_prompt(qs, ks, vs)
        o_mem = _mem_prompt(qc, mk, mv)
        xp = _out_proj([o_moba, o_sb, o_mem], hg, w_out_b, xp, row(ln2_g[l]), row(ln2_b[l]))

        qm2, km2, vm2, qs2, ks2, vs2, qc2 = _proj(xs, w_in_b, _IN_HEADS, PROJ_GROUP, 1)
        pad = lambda a: _sample_rows(a, nseq, t_new)
        o8 = _sample_attention(l, page_table, slope_rows, pad(qm2), pad(km2), pad(vm2),
                               pad(qs2), pad(ks2), pad(vs2), pad(qc2), memk, memv, pools)
        o2 = o8[:, :, :t_new, :].transpose(1, 0, 2, 3).reshape(1, N_HEADS, nseq * t_new, HEAD_DIM)
        xs = _out_proj([o2], hg, w_out_b, xs, row(ln2_g[l]), row(ln2_b[l]))

        xp = _ffn(xp, *f2, row(ln3_g[l]), row(ln3_b[l]))
        xs = _ffn(xs, *f2, row(ln3_g[l]), row(ln3_b[l]))

        for dst, val in zip(leaves, (km, vm, ks, vs, mk, mv, km2, vm2, ks2, vs2)):
            dst.append(val)

    def prompt_leaf(vals):
        return jnp.stack(vals).transpose(0, 1, 3, 2, 4)

    def sample_leaf(vals):
        x = jnp.stack(vals)
        return x.reshape(DEPTH, x.shape[2], nseq, t_new, HEAD_DIM).transpose(0, 2, 3, 1, 4)

    return (
        xp.reshape(bsz, seq, D_MODEL), xs.reshape(nseq, t_new, D_MODEL),
        prompt_leaf(leaves[0]), prompt_leaf(leaves[1]), prompt_leaf(leaves[2]), prompt_leaf(leaves[3]),
        prompt_leaf(leaves[4]), prompt_leaf(leaves[5]),
        sample_leaf(leaves[6]), sample_leaf(leaves[7]), sample_leaf(leaves[8]), sample_leaf(leaves[9]),
    )
```

```python
import functools

import jax
import jax.numpy as jnp
from jax import lax
from jax.experimental import pallas as pl
from jax.experimental.pallas import tpu as pltpu

F32 = jnp.float32
BF16 = jnp.bfloat16

D_MODEL = 2048
DEPTH = 2
HEAD_DIM = 128
H_MOBA = 6
H_SB = 6
H_MEM = 4
N_HEADS = H_MOBA + H_SB + H_MEM
W_MOBA = H_MOBA * HEAD_DIM
W_SB = H_SB * HEAD_DIM
W_MEM = H_MEM * HEAD_DIM
MOBA_BLOCK = 256
MOBA_TOPK = 3
PAGE_SIZE = 128
D_FF = 5504
DEEPNORM_ALPHA = (2.0 * DEPTH) ** 0.25
LN_EPS = 1e-5
RMS_EPS = 1e-6
NEG_INF = -1e30
SCALE = HEAD_DIM ** -0.5

LANES_V7X = 128
SUBLANES_V7X = 8
VMEM_BYTES_V7X = 64 * 1024 * 1024

FF_CHUNK = 512
D_FF_PAD = -(-D_FF // FF_CHUNK) * FF_CHUNK
ROW_TILE = 512
CAST_ROWS = 256
PROJ_GROUP = W_MOBA
SAMPLE_ROWS = SUBLANES_V7X

def _vmem_limit(nbytes):
    return int(min(nbytes + (8 << 20), VMEM_BYTES_V7X - (6 << 20)))


def _dot(a, b):
    return jnp.dot(a, b, preferred_element_type=F32)


def _dot_nt(a, b, precision=None):
    return lax.dot_general(a, b, (((1,), (1,)), ((), ())),
                           preferred_element_type=F32, precision=precision)


def _layer_norm(y, g, b):
    mu = jnp.mean(y, axis=-1, keepdims=True)
    d = y - mu
    var = jnp.mean(d * d, axis=-1, keepdims=True)
    return d * lax.rsqrt(var + LN_EPS) * g + b


def _rms_norm(o):
    return o * lax.rsqrt(jnp.mean(o * o, axis=-1, keepdims=True) + RMS_EPS)


def _log_sigmoid(z):
    return jnp.minimum(z, 0.0) - jnp.log1p(jnp.exp(-jnp.abs(z)))


def _suffix_matrix(n, keys_on_rows=False):
    r = lax.broadcasted_iota(jnp.int32, (n, n), 0)
    c = lax.broadcasted_iota(jnp.int32, (n, n), 1)
    return jnp.where((c > r) if keys_on_rows else (r > c), 1.0, 0.0).astype(BF16)


def _split_bf16(x):
    hi = x.astype(BF16)
    lo = (x - hi.astype(F32)).astype(BF16)
    return hi, lo


def _suffix_sum(x, t):
    hi, lo = _split_bf16(x)
    return _dot(hi, t) + _dot(lo, t)


def _topk_select(g, q_blk, axis, n_trailing=0):
    idx = lax.broadcasted_iota(jnp.int32, g.shape, axis)
    selm = jnp.zeros(g.shape, F32)
    for j in range(g.shape[axis]):
        gj = lax.slice_in_dim(g, j, j + 1, axis=axis)
        beats = jnp.where(g > gj, 1.0, jnp.where((g == gj) & (idx < j), 1.0, 0.0))
        rank = jnp.sum(beats, axis=axis, keepdims=True)
        if n_trailing:
            rank = rank + jnp.where(gj < NEG_INF, float(n_trailing), 0.0)
        selm = jnp.where(idx == j, jnp.where(rank < MOBA_TOPK, 1.0, 0.0), selm)
    return jnp.where(idx < q_blk, selm, 0.0)


def _ffn_kernel(x_ref, wg_ref, wu_ref, wd_ref, g_ref, b_ref, o_ref, xb_ref, acc_ref):
    j = pl.program_id(1)

    @pl.when(j == 0)
    def _():
        xb_ref[...] = x_ref[...].astype(BF16)
        acc_ref[...] = jnp.zeros_like(acc_ref)

    xb = xb_ref[...]
    gate = _dot(xb, wg_ref[...])
    up = _dot(xb, wu_ref[...])
    act = (gate * jax.nn.sigmoid(gate)) * up
    acc_ref[...] += _dot(act.astype(BF16), wd_ref[...])

    @pl.when(j == pl.num_programs(1) - 1)
    def _():
        y = DEEPNORM_ALPHA * x_ref[...] + 0.5 * acc_ref[...]
        o_ref[...] = _layer_norm(y, g_ref[...], b_ref[...])


def _ffn(x, w_gate_up, w_down, layer, g, b):
    m = x.shape[0]
    tm, tf = ROW_TILE, FF_CHUNK
    n_chunks = D_FF_PAD // tf
    nbytes = (2 * 2 * tm * D_MODEL * 4 + tm * D_MODEL * (2 + 4)
              + 2 * 3 * D_MODEL * tf * 2 + 3 * tm * tf * 4)
    return pl.pallas_call(
        _ffn_kernel,
        grid=(m // tm, n_chunks),
        in_specs=[
            pl.BlockSpec((tm, D_MODEL), lambda i, j: (i, 0)),
            pl.BlockSpec((None, D_MODEL, tf), lambda i, j: (layer, 0, j)),
            pl.BlockSpec((None, D_MODEL, tf), lambda i, j: (layer, 0, n_chunks + j)),
            pl.BlockSpec((None, tf, D_MODEL), lambda i, j: (layer, j, 0)),
            pl.BlockSpec((1, D_MODEL), lambda i, j: (0, 0)),
            pl.BlockSpec((1, D_MODEL), lambda i, j: (0, 0)),
        ],
        out_specs=pl.BlockSpec((tm, D_MODEL), lambda i, j: (i, 0)),
        out_shape=jax.ShapeDtypeStruct((m, D_MODEL), F32),
        scratch_shapes=[pltpu.VMEM((tm, D_MODEL), BF16), pltpu.VMEM((tm, D_MODEL), F32)],
        compiler_params=pltpu.CompilerParams(
            dimension_semantics=("parallel", "arbitrary"),
            vmem_limit_bytes=_vmem_limit(nbytes)),
        name="ffn",
    )(x, w_gate_up, w_gate_up, w_down, g, b)


def _cast_cols_kernel(x_ref, o_ref, *, segments):
    covered = 0
    for dst, src, width in segments:
        if dst > covered:
            o_ref[:, covered:dst] = jnp.zeros((o_ref.shape[0], dst - covered), BF16)
        o_ref[:, dst:dst + width] = x_ref[:, src:src + width].astype(BF16)
        covered = dst + width
    if covered < o_ref.shape[1]:
        o_ref[:, covered:] = jnp.zeros((o_ref.shape[0], o_ref.shape[1] - covered), BF16)


def _cast_cols(w, out_cols, segments):
    depth, n_rows, n_cols = w.shape
    tr = CAST_ROWS
    nbytes = 2 * tr * (n_cols * 4 + out_cols * 2)
    return pl.pallas_call(
        functools.partial(_cast_cols_kernel, segments=segments),
        grid=(depth, n_rows // tr),
        in_specs=[pl.BlockSpec((None, tr, n_cols), lambda l, i: (l, i, 0))],
        out_specs=pl.BlockSpec((None, tr, out_cols), lambda l, i: (l, i, 0)),
        out_shape=jax.ShapeDtypeStruct((depth, n_rows, out_cols), BF16),
        compiler_params=pltpu.CompilerParams(
            dimension_semantics=("parallel", "parallel"),
            vmem_limit_bytes=_vmem_limit(nbytes)),
        name="cast_cols",
    )(w)


def _cast_rows_kernel(x_ref, o_ref):
    n = x_ref.shape[0]
    o_ref[0:n, :] = x_ref[...].astype(BF16)
    o_ref[n:, :] = jnp.zeros((o_ref.shape[0] - n, o_ref.shape[1]), BF16)


def _cast_pad_rows(w, out_rows):
    depth, n_rows, n_cols = w.shape
    tc = CAST_ROWS
    nbytes = 2 * tc * (n_rows * 4 + out_rows * 2)
    return pl.pallas_call(
        _cast_rows_kernel,
        grid=(depth, n_cols // tc),
        in_specs=[pl.BlockSpec((None, n_rows, tc), lambda l, i: (l, 0, i))],
        out_specs=pl.BlockSpec((None, out_rows, tc), lambda l, i: (l, 0, i)),
        out_shape=jax.ShapeDtypeStruct((depth, out_rows, n_cols), BF16),
        compiler_params=pltpu.CompilerParams(
            dimension_semantics=("parallel", "parallel"),
            vmem_limit_bytes=_vmem_limit(nbytes)),
        name="cast_pad_rows",
    )(w)


def _prep_ffn(w_in, w_down):
    w_gate_up = _cast_cols(w_in, 2 * D_FF_PAD, [(0, 0, D_FF), (D_FF_PAD, D_FF, D_FF)])
    return w_gate_up, _cast_pad_rows(w_down, D_FF_PAD)


def _proj_kernel(x_ref, w_ref, *refs):
    out_refs, xb_ref = refs[:-1], refs[-1]
    g = pl.program_id(1)

    @pl.when(g == 0)
    def _():
        xb_ref[...] = x_ref[...].astype(BF16)

    res = _dot(xb_ref[...], w_ref[...])
    for k, o_ref in enumerate(out_refs):
        @pl.when(g == k)
        def _(o_ref=o_ref):
            for h in range(o_ref.shape[1]):
                o_ref[0, h] = res[:, h * HEAD_DIM:(h + 1) * HEAD_DIM]


def _proj(x, w, layer, heads, group, bsz):
    m = x.shape[0]
    seq = m // bsz
    tm = min(ROW_TILE, seq)
    n_st = seq // tm
    nbytes = (2 * tm * D_MODEL * 4 + tm * D_MODEL * 2 + 2 * D_MODEL * group * 2
              + 2 * sum(heads) * HEAD_DIM * tm * 4 + tm * group * 4)
    return pl.pallas_call(
        _proj_kernel,
        grid=(m // tm, len(heads)),
        in_specs=[
            pl.BlockSpec((tm, D_MODEL), lambda i, g: (i, 0)),
            pl.BlockSpec((None, D_MODEL, group), lambda i, g: (layer, 0, g)),
        ],
        out_specs=[pl.BlockSpec((1, nh, tm, HEAD_DIM), lambda i, g: (i // n_st, 0, i % n_st, 0))
                   for nh in heads],
        out_shape=[jax.ShapeDtypeStruct((bsz, nh, seq, HEAD_DIM), F32) for nh in heads],
        scratch_shapes=[pltpu.VMEM((tm, D_MODEL), BF16)],
        compiler_params=pltpu.CompilerParams(
            dimension_semantics=("parallel", "arbitrary"),
            vmem_limit_bytes=_vmem_limit(nbytes)),
        name="proj",
    )(x, w)


_IN_HEADS = (H_MOBA,) * 3 + (H_SB,) * 3 + (H_MEM,)


def _prep_proj_in(w_in):
    n_cols = w_in.shape[2]
    return _cast_cols(w_in, n_cols + PROJ_GROUP - W_MEM, [(0, 0, n_cols)])


def _stage_kv(k_ref, v_ref, kb_ref, vt_ref):
    blk = MOBA_BLOCK
    kb_ref[...] = k_ref[0, 0].astype(BF16)
    for j in range(k_ref.shape[2] // blk):
        vt_ref[:, j * blk:(j + 1) * blk] = v_ref[0, 0, j * blk:(j + 1) * blk, :].T.astype(BF16)


def _rms_norm_cols(o):
    return o * lax.rsqrt(jnp.mean(o * o, axis=0, keepdims=True) + RMS_EPS)


def _software_pipeline(items, stages):
    n, depth = len(items), len(stages)
    results = {}
    for step in range(n + depth - 1):
        for k, stage in enumerate(stages):
            i = step - k
            if 0 <= i < n:
                results[i] = stage(items[i], results.get(i))


def _tile(t):
    return slice(t * MOBA_BLOCK, (t + 1) * MOBA_BLOCK)


def _moba_prompt_kernel(slopes_ref, q_ref, k_ref, v_ref, o_ref, kb_ref, vt_ref, qb_ref, kmean_ref):
    h = pl.program_id(1)
    blk = MOBA_BLOCK
    n_blocks = k_ref.shape[2] // blk

    _stage_kv(k_ref, v_ref, kb_ref, vt_ref)
    qb_ref[...] = q_ref[0, 0].astype(BF16)
    for j in range(n_blocks):
        kmean_ref[j:j + 1, :] = jnp.mean(k_ref[0, 0, _tile(j), :], axis=0, keepdims=True)

    slope = slopes_ref[h]
    gate = _dot_nt(kmean_ref[...], q_ref[0, 0], precision=lax.Precision.HIGHEST)
    jblk = lax.broadcasted_iota(jnp.int32, (n_blocks, blk), 0)
    key = lax.broadcasted_iota(jnp.int32, (blk, blk), 0)
    qi = lax.broadcasted_iota(jnp.int32, (blk, blk), 1)
    dist_own = (qi - key).astype(F32)
    selm, stats, accs = {}, {}, {}

    def scores(item, _):
        qt, j = item
        return _dot_nt(kb_ref[_tile(j), :], qb_ref[_tile(qt), :])

    def softmax(item, s):
        qt, j = item
        if j == qt:
            if qt > 0:
                selm[qt] = _topk_select(jnp.where(jblk < qt, gate[:, _tile(qt)], NEG_INF), qt, axis=0)
            s = s * SCALE - slope * dist_own
            s = jnp.where(key <= qi, s, NEG_INF)
            m = jnp.max(s, axis=0, keepdims=True)
            a = None
            p = jnp.exp(s - m)
            l = jnp.sum(p, axis=0, keepdims=True)
        else:
            m_old, l_old = stats[qt]
            s = s * SCALE - slope * (dist_own + float((qt - j) * blk))
            s = jnp.where(selm[qt][j:j + 1, :] > 0.0, s, NEG_INF)
            m = jnp.maximum(m_old, jnp.max(s, axis=0, keepdims=True))
            a = jnp.exp(m_old - m)
            p = jnp.exp(s - m)
            l = a * l_old + jnp.sum(p, axis=0, keepdims=True)
        stats[qt] = (m, l)
        return a, l, _dot(vt_ref[:, _tile(j)], p.astype(BF16))

    def accumulate(item, res):
        qt, j = item
        a, l, pv = res
        accs[qt] = pv if a is None else a * accs[qt] + pv
        if j == qt - 1 or qt == 0:
            o_ref[0, 0, _tile(qt), :] = _rms_norm_cols(accs.pop(qt) / l).T

    items = [(qt, j) for qt in range(n_blocks) for j in [qt] + list(range(qt))]
    _software_pipeline(items, [scores, softmax, accumulate])


def _head_attention_call(kernel_fn, name, n_heads, extra_inputs, extra_specs, extra_scratch, q, k, v):
    bsz, _, seq, _ = q.shape
    spec = pl.BlockSpec((1, 1, seq, HEAD_DIM), lambda b, h: (b, h, 0, 0))
    nbytes = 4 * 2 * seq * HEAD_DIM * 4 + 3 * seq * HEAD_DIM * 2 + (16 << 20)
    return pl.pallas_call(
        kernel_fn,
        grid=(bsz, n_heads),
        in_specs=extra_specs + [spec, spec, spec],
        out_specs=spec,
        out_shape=jax.ShapeDtypeStruct(q.shape, F32),
        scratch_shapes=[pltpu.VMEM((seq, HEAD_DIM), BF16), pltpu.VMEM((HEAD_DIM, seq), BF16),
                        pltpu.VMEM((seq, HEAD_DIM), BF16)] + extra_scratch,
        compiler_params=pltpu.CompilerParams(
            dimension_semantics=("parallel", "parallel"),
            vmem_limit_bytes=_vmem_limit(nbytes)),
        name=name,
    )(*extra_inputs, q, k, v)


def _moba_prompt(q, k, v, slopes):
    n_blocks = q.shape[2] // MOBA_BLOCK
    return _head_attention_call(
        _moba_prompt_kernel, "moba_prompt", H_MOBA, [slopes], [pl.BlockSpec(memory_space=pltpu.SMEM)],
        [pltpu.VMEM((n_blocks, HEAD_DIM), F32)], q, k, v)


def _sb_prompt_kernel(q_ref, k_ref, v_ref, o_ref, kb_ref, vt_ref, qb_ref, t_ref):
    blk = MOBA_BLOCK
    n_tiles = k_ref.shape[2] // blk

    _stage_kv(k_ref, v_ref, kb_ref, vt_ref)
    qb_ref[...] = q_ref[0, 0].astype(BF16)
    t_ref[...] = _suffix_matrix(blk, keys_on_rows=True)
    key = lax.broadcasted_iota(jnp.int32, (blk, blk), 0)
    qi = lax.broadcasted_iota(jnp.int32, (blk, blk), 1)
    own_valid = key < qi
    carries, accs = {}, {}

    def scores(item, _):
        qt, j = item
        return _dot_nt(kb_ref[_tile(j), :], qb_ref[_tile(qt), :])

    def suffix_sums(item, z):
        qt, j = item
        z = z * SCALE
        ls = _log_sigmoid(z)
        lk = ls - z
        if j == qt:
            lk = jnp.where(own_valid, lk, 0.0)
        hi, lo = _split_bf16(lk)
        carry = carries.get(qt)
        total = jnp.sum(lk, axis=0, keepdims=True)
        carries[qt] = total if carry is None else carry + total
        return ls, carry, _dot(t_ref[...], hi) + _dot(t_ref[...], lo)

    def weigh_values(item, res):
        qt, j = item
        ls, carry, after = res
        if carry is not None:
            after = after + carry
        w = jnp.exp(ls + after)
        if j == qt:
            w = jnp.where(own_valid, w, 0.0)
        return _dot(vt_ref[:, _tile(j)], w.astype(BF16))

    def accumulate(item, out):
        qt, j = item
        accs[qt] = out if j == qt else accs[qt] + out
        if j == 0:
            o_ref[0, 0, _tile(qt), :] = _rms_norm_cols(accs.pop(qt)).T

    items = [(qt, j) for qt in range(n_tiles) for j in range(qt, -1, -1)]
    _software_pipeline(items, [scores, suffix_sums, weigh_values, accumulate])


def _sb_prompt(q, k, v):
    return _head_attention_call(_sb_prompt_kernel, "sb_prompt", H_SB, [], [],
                                [pltpu.VMEM((MOBA_BLOCK, MOBA_BLOCK), BF16)], q, k, v)


def _mem_prompt_kernel(q_ref, mk_ref, mv_ref, o_ref):
    for h in range(H_MEM):
        s = _dot_nt(q_ref[0, h].astype(BF16), mk_ref[0, h].astype(BF16)) * SCALE
        p = jnp.exp(s - jnp.max(s, axis=-1, keepdims=True))
        l = jnp.sum(p, axis=-1, keepdims=True)
        o = _dot(p.astype(BF16), mv_ref[0, h].astype(BF16)) / l
        o_ref[0, h] = _rms_norm(o)


def _mem_prompt(q, mk, mv):
    bsz, _, seq, _ = q.shape
    n_mem = mk.shape[2]
    tq = ROW_TILE
    nbytes = 2 * 2 * tq * W_MEM * 4 + 2 * 2 * n_mem * W_MEM * 4 + (4 << 20)
    q_spec = pl.BlockSpec((1, H_MEM, tq, HEAD_DIM), lambda b, t: (b, 0, t, 0))
    m_spec = pl.BlockSpec((1, H_MEM, n_mem, HEAD_DIM), lambda b, t: (b, 0, 0, 0))
    return pl.pallas_call(
        _mem_prompt_kernel,
        grid=(bsz, seq // tq),
        in_specs=[q_spec, m_spec, m_spec],
        out_specs=q_spec,
        out_shape=jax.ShapeDtypeStruct(q.shape, F32),
        compiler_params=pltpu.CompilerParams(
            dimension_semantics=("parallel", "parallel"),
            vmem_limit_bytes=_vmem_limit(nbytes)),
        name="mem_prompt",
    )(q, mk, mv)


def _out_kernel(*refs, n_pieces):
    o_refs = refs[:n_pieces]
    hg_ref, w_ref, x_ref, g_ref, b_ref, out_ref = refs[n_pieces:]
    acc = DEEPNORM_ALPHA * x_ref[...]
    off = 0
    for o_ref in o_refs:
        wd = o_ref.shape[1] * HEAD_DIM
        o = jnp.concatenate([o_ref[0, h] for h in range(o_ref.shape[1])], axis=1)
        ob = (o * hg_ref[:, off:off + wd]).astype(BF16)
        acc = acc + _dot(ob, w_ref[off:off + wd, :])
        off += wd
    out_ref[...] = _layer_norm(acc, g_ref[...], b_ref[...])


def _out_proj(pieces, head_g, w_out, layer, x, g, b):
    m = x.shape[0]
    tm = ROW_TILE
    n_st = pieces[0].shape[2] // tm
    nbytes = 2 * 3 * tm * D_MODEL * 4 + 2 * D_MODEL * D_MODEL * 2 + 2 * tm * D_MODEL * 4
    row = lambda i: (i, 0)
    const = lambda i: (0, 0)
    return pl.pallas_call(
        functools.partial(_out_kernel, n_pieces=len(pieces)),
        grid=(m // tm,),
        in_specs=[pl.BlockSpec((1, p.shape[1], tm, HEAD_DIM), lambda i: (i // n_st, 0, i % n_st, 0))
                  for p in pieces] + [
            pl.BlockSpec((1, D_MODEL), const),
            pl.BlockSpec((None, D_MODEL, D_MODEL), lambda i: (layer, 0, 0)),
            pl.BlockSpec((tm, D_MODEL), row),
            pl.BlockSpec((1, D_MODEL), const),
            pl.BlockSpec((1, D_MODEL), const),
        ],
        out_specs=pl.BlockSpec((tm, D_MODEL), row),
        out_shape=jax.ShapeDtypeStruct((m, D_MODEL), F32),
        compiler_params=pltpu.CompilerParams(
            dimension_semantics=("parallel",),
            vmem_limit_bytes=_vmem_limit(nbytes)),
        name="out_proj",
    )(*pieces, head_g, w_out, x, g, b)


def _sample_kernel(pt_ref, slope_ref, qm_ref, kmn_ref, vmn_ref, qs_ref, ksn_ref, vsn_ref, qc_ref,
                   memk_ref, memv_ref, ckm_ref, cvm_ref, cks_ref, cvs_ref, o_ref,
                   buf_ref, sem_ref, s_ref, kmean_ref, tmat_ref,
                   *, layer, n_pages, past_len):
    b = pl.program_id(0)
    nb = pl.num_programs(0)
    pools = (ckm_ref, cvm_ref, cks_ref, cvs_ref)
    blk = MOBA_BLOCK
    n_blocks = past_len // blk
    n_tiles = n_blocks + 1
    rows = SAMPLE_ROWS
    n_rows = s_ref.shape[0]

    def page_copy(pool, seq, p):
        page = pt_ref[seq * n_pages + p]
        return pltpu.make_async_copy(pools[pool].at[layer, page],
                                     buf_ref.at[pool, :, pl.ds(p * PAGE_SIZE, PAGE_SIZE), :],
                                     sem_ref.at[pool])

    def start_pool(pool, seq):
        for p in range(n_pages):
            page_copy(pool, seq, p).start()

    def wait_pool(pool, seq):
        for p in range(n_pages):
            page_copy(pool, seq, p).wait()

    def start_next(pool):
        @pl.when(b + 1 < nb)
        def _():
            start_pool(pool, b + 1)

    def head_rows(h):
        return slice(h * rows, (h + 1) * rows)

    def tile_cols(j):
        return slice(j * blk, (j + 1) * blk)

    @pl.when(b == 0)
    def _():
        buf_ref[:, :, past_len:, :] = jnp.zeros((4, buf_ref.shape[1], blk, HEAD_DIM), F32)
        tmat_ref[...] = _suffix_matrix(blk)
        for pool in range(4):
            start_pool(pool, b)

    for pool, new in enumerate((kmn_ref, vmn_ref, ksn_ref, vsn_ref)):
        buf_ref[pool, :, past_len:past_len + rows, :] = new[0]

    t_q = lax.broadcasted_iota(jnp.int32, (n_rows, blk), 0) % rows
    k_in = lax.broadcasted_iota(jnp.int32, (n_rows, blk), 1)
    slope = slope_ref[...]

    def weighted_values(pool):
        for h in range(buf_ref.shape[1]):
            w = s_ref[head_rows(h), :].astype(BF16)
            yield h, _dot(w, buf_ref[pool, h].astype(BF16))

    wait_pool(0, b)
    gates = []
    for h in range(H_MOBA):
        q = qm_ref[0, h]
        for j in range(n_blocks):
            kmean_ref[h * n_blocks + j:h * n_blocks + j + 1, :] = jnp.mean(
                buf_ref[0, h, tile_cols(j), :], axis=0, keepdims=True)
        s_ref[head_rows(h), :] = _dot_nt(q.astype(BF16), buf_ref[0, h].astype(BF16))
        gates.append(_dot_nt(q, kmean_ref[h * n_blocks:(h + 1) * n_blocks, :],
                             precision=lax.Precision.HIGHEST))
    start_next(0)
    selm = _topk_select(jnp.concatenate(gates, axis=0), n_blocks, axis=1, n_trailing=1)

    tiles = []
    for j in range(n_tiles):
        dist = (t_q - k_in + (past_len - j * blk)).astype(F32)
        s = s_ref[:, tile_cols(j)] * SCALE - slope * dist
        if j < n_blocks:
            s = jnp.where(selm[:, j:j + 1] > 0.0, s, NEG_INF)
        else:
            s = jnp.where(k_in <= t_q, s, NEG_INF)
        tiles.append(s)
    m = functools.reduce(jnp.maximum, [jnp.max(s, axis=-1, keepdims=True) for s in tiles])
    l = jnp.zeros((n_rows, 1), F32)
    for j in range(n_tiles):
        p = jnp.exp(tiles[j] - m)
        l = l + jnp.sum(p, axis=-1, keepdims=True)
        s_ref[:, tile_cols(j)] = p
    wait_pool(1, b)
    for h, out in weighted_values(1):
        o_ref[0, h] = _rms_norm(out / l[head_rows(h), :])
    start_next(1)

    wait_pool(2, b)
    for h in range(H_SB):
        s_ref[head_rows(h), :] = _dot_nt(qs_ref[0, h].astype(BF16), buf_ref[2, h].astype(BF16))
    start_next(2)
    new_valid = k_in < t_q
    log_beta, log_keep = [], []
    for j in range(n_tiles):
        z = s_ref[:, tile_cols(j)] * SCALE
        ls = _log_sigmoid(z)
        lk = ls - z
        if j == n_blocks:
            lk = jnp.where(new_valid, lk, 0.0)
        log_beta.append(ls)
        log_keep.append(lk)
    after = _suffix_sum(jnp.concatenate(log_keep, axis=0), tmat_ref[...])
    carry = jnp.zeros((n_rows, 1), F32)
    for j in reversed(range(n_tiles)):
        w = jnp.exp(log_beta[j] + after[j * n_rows:(j + 1) * n_rows, :] + carry)
        if j == n_blocks:
            w = jnp.where(new_valid, w, 0.0)
        s_ref[:, tile_cols(j)] = w
        carry = carry + jnp.sum(log_keep[j], axis=-1, keepdims=True)
    wait_pool(3, b)
    for h, out in weighted_values(3):
        o_ref[0, H_MOBA + h] = _rms_norm(out)
    start_next(3)

    n_keys = memk_ref.shape[2]
    qc = jnp.concatenate([qc_ref[0, h] for h in range(H_MEM)], axis=0).astype(BF16)
    s = _dot_nt(qc, memk_ref[0, 0].astype(BF16)) * SCALE
    row_head = lax.broadcasted_iota(jnp.int32, (H_MEM * rows, n_keys), 0) // rows
    key_head = lax.broadcasted_iota(jnp.int32, (H_MEM * rows, n_keys), 1) % H_MEM
    s = jnp.where(row_head == key_head, s, NEG_INF)
    p = jnp.exp(s - jnp.max(s, axis=-1, keepdims=True))
    l = jnp.sum(p, axis=-1, keepdims=True)
    o = _dot(p.astype(BF16), memv_ref[0, 0].astype(BF16)) / l
    for h in range(H_MEM):
        o_ref[0, H_MOBA + H_SB + h] = _rms_norm(o[h * rows:(h + 1) * rows, :])


def _sample_attention(layer, page_table, slope_rows, qm, kmn, vmn, qs, ksn, vsn, qc, memk, memv, pools):
    nseq, n_pages = page_table.shape
    past_len = n_pages * PAGE_SIZE
    n_keys = memk.shape[2]
    n_heads_pool = pools[0].shape[2]

    def heads_spec(nh):
        return pl.BlockSpec((1, nh, SAMPLE_ROWS, HEAD_DIM), lambda b, pt: (b, 0, 0, 0))

    mem_spec = pl.BlockSpec((1, 1, n_keys, HEAD_DIM), lambda b, pt: (layer, b, 0, 0))
    any_spec = pl.BlockSpec(memory_space=pl.ANY)
    key_len = past_len + MOBA_BLOCK
    n_rows = n_heads_pool * SAMPLE_ROWS
    n_blocks = past_len // MOBA_BLOCK
    nbytes = (4 * n_heads_pool * key_len * HEAD_DIM * 4 + 2 * 2 * n_keys * HEAD_DIM * 4
              + n_rows * key_len * 4 + (8 << 20))
    grid_spec = pltpu.PrefetchScalarGridSpec(
        num_scalar_prefetch=1,
        grid=(nseq,),
        in_specs=[pl.BlockSpec((n_rows, 1), lambda b, pt: (0, 0)),
                  heads_spec(H_MOBA), heads_spec(H_MOBA), heads_spec(H_MOBA),
                  heads_spec(H_SB), heads_spec(H_SB), heads_spec(H_SB), heads_spec(H_MEM),
                  mem_spec, mem_spec, any_spec, any_spec, any_spec, any_spec],
        out_specs=pl.BlockSpec((1, N_HEADS, SAMPLE_ROWS, HEAD_DIM), lambda b, pt: (b, 0, 0, 0)),
        scratch_shapes=[
            pltpu.VMEM((4, n_heads_pool, key_len, HEAD_DIM), F32),
            pltpu.SemaphoreType.DMA((4,)),
            pltpu.VMEM((n_rows, key_len), F32),
            pltpu.VMEM((n_heads_pool * n_blocks, HEAD_DIM), F32),
            pltpu.VMEM((MOBA_BLOCK, MOBA_BLOCK), BF16),
        ],
    )
    return pl.pallas_call(
        functools.partial(_sample_kernel, layer=layer, n_pages=n_pages, past_len=past_len),
        grid_spec=grid_spec,
        out_shape=jax.ShapeDtypeStruct((nseq, N_HEADS, SAMPLE_ROWS, HEAD_DIM), F32),
        compiler_params=pltpu.CompilerParams(
            dimension_semantics=("arbitrary",),
            vmem_limit_bytes=_vmem_limit(nbytes)),
        name="sample_attention",
    )(page_table.reshape(-1), slope_rows, qm, kmn, vmn, qs, ksn, vsn, qc, memk, memv, *pools)


def _sample_rows(x, nseq, t):
    nh = x.shape[1]
    x = x.reshape(nh, nseq, t, HEAD_DIM).transpose(1, 0, 2, 3)
    return jnp.pad(x, ((0, 0), (0, 0), (0, SAMPLE_ROWS - t), (0, 0)))


def kernel(x_prompt, x_sample, cache_moba_k, cache_moba_v, cache_sb_k, cache_sb_v, cache_mem_k, cache_mem_v, page_table, mem_prompt, ln1_g, ln1_b, ln2_g, ln2_b, ln3_g, ln3_b, ffn1_w_in, ffn1_w_down, w_in, w_mem_kv, head_g, w_out, ffn2_w_in, ffn2_w_down):
    bsz, seq, _ = x_prompt.shape
    nseq, t_new, _ = x_sample.shape
    n_mem = mem_prompt.shape[1]
    assert seq % MOBA_BLOCK == 0 and seq % ROW_TILE == 0 and t_new <= SAMPLE_ROWS // 2
    assert (page_table.shape[1] * PAGE_SIZE) % MOBA_BLOCK == 0 and (nseq * t_new) % ROW_TILE == 0

    slopes = jnp.exp2(-8.0 * jnp.arange(1, H_MOBA + 1, dtype=F32) / H_MOBA)
    slope_rows = jnp.repeat(slopes, SAMPLE_ROWS).reshape(H_MOBA * SAMPLE_ROWS, 1)

    xp = x_prompt.reshape(bsz * seq, D_MODEL)
    xs = x_sample.reshape(nseq * t_new, D_MODEL)
    mem2d = mem_prompt.reshape(bsz * n_mem, D_MODEL)
    pools = [c.transpose(0, 1, 3, 2, 4) for c in (cache_moba_k, cache_moba_v, cache_sb_k, cache_sb_v)]
    memk = cache_mem_k.reshape(DEPTH, nseq, n_mem * H_MEM, HEAD_DIM)
    memv = cache_mem_v.reshape(DEPTH, nseq, n_mem * H_MEM, HEAD_DIM)
    row = lambda v: v.reshape(1, -1)

    f1 = _prep_ffn(ffn1_w_in, ffn1_w_down)
    f2 = _prep_ffn(ffn2_w_in, ffn2_w_down)
    w_in_b = _prep_proj_in(w_in)
    w_mem_b = w_mem_kv.astype(BF16)
    w_out_b = w_out.astype(BF16)

    leaves = [[] for _ in range(10)]
    for l in range(DEPTH):
        hg = row(head_g[l])

        xp = _ffn(xp, *f1, l, row(ln1_g[l]), row(ln1_b[l]))
        xs = _ffn(xs, *f1, l, row(ln1_g[l]), row(ln1_b[l]))

        qm, km, vm, qs, ks, vs, qc = _proj(xp, w_in_b, l, _IN_HEADS, PROJ_GROUP, bsz)
        mk, mv = _proj(mem2d, w_mem_b, l, (H_MEM, H_MEM), W_MEM, bsz)
        o_moba = _moba_prompt(qm, km, vm, slopes)
        o_sb = _sb_prompt(qs, ks, vs)
        o_mem = _mem_prompt(qc, mk, mv)
        xp = _out_proj([o_moba, o_sb, o_mem], hg, w_out_b, l, xp, row(ln2_g[l]), row(ln2_b[l]))

        qm2, km2, vm2, qs2, ks2, vs2, qc2 = _proj(xs, w_in_b, l, _IN_HEADS, PROJ_GROUP, 1)
        pad = lambda a: _sample_rows(a, nseq, t_new)
        o8 = _sample_attention(l, page_table, slope_rows, pad(qm2), pad(km2), pad(vm2),
                               pad(qs2), pad(ks2), pad(vs2), pad(qc2), memk, memv, pools)
        o2 = o8[:, :, :t_new, :].transpose(1, 0, 2, 3).reshape(1, N_HEADS, nseq * t_new, HEAD_DIM)
        xs = _out_proj([o2], hg, w_out_b, l, xs, row(ln2_g[l]), row(ln2_b[l]))

        xp = _ffn(xp, *f2, l, row(ln3_g[l]), row(ln3_b[l]))
        xs = _ffn(xs, *f2, l, row(ln3_g[l]), row(ln3_b[l]))

        for dst, val in zip(leaves, (km, vm, ks, vs, mk, mv, km2, vm2, ks2, vs2)):
            dst.append(val)

    def prompt_leaf(vals):
        return jnp.stack(vals).transpose(0, 1, 3, 2, 4)

    def sample_leaf(vals):
        x = jnp.stack(vals)
        return x.reshape(DEPTH, x.shape[2], nseq, t_new, HEAD_DIM).transpose(0, 2, 3, 1, 4)

    return (
        xp.reshape(bsz, seq, D_MODEL), xs.reshape(nseq, t_new, D_MODEL),
        prompt_leaf(leaves[0]), prompt_leaf(leaves[1]), prompt_leaf(leaves[2]), prompt_leaf(leaves[3]),
        prompt_leaf(leaves[4]), prompt_leaf(leaves[5]),
        sample_leaf(leaves[6]), sample_leaf(leaves[7]), sample_leaf(leaves[8]), sample_leaf(leaves[9]),
    )
```
